```python
import math
import jax, jax.numpy as jnp
from jax import lax
import numpy as np

D_MODEL = 1024
BATCH = 2
SEQ = 8192
DEPTH = 2
DEC_BATCH = 32
DEC_SEQ = 1
PAST_LEN = 8192
PAGE_SIZE = 128

N_MIXERS = 2
N_RET_LAYERS = (DEPTH + 1) // 2
N_DIFF_LAYERS = DEPTH // 2
D_PLE = 256
D_FF = 4 * D_MODEL
EPS = 1e-6
NEG_INF = -1e30

RET_HEADS = 4
RET_DK = D_MODEL // RET_HEADS
RET_DV = 2 * RET_DK
RET_CHUNK = 128
ROPE_BASE = 10000.0

DIFF_HEADS = 8
DIFF_DH = D_MODEL // (2 * DIFF_HEADS)
DIFF_DV = 2 * DIFF_DH
Q_BLOCK = 128

kernel_name = 'retnet_diffattn_hybrid_step'


def _rmsnorm(x, g):
    xf = x.astype(jnp.float32)
    y = xf * lax.rsqrt(jnp.mean(xf * xf, axis=-1, keepdims=True) + EPS)
    return (y * g.astype(jnp.float32)).astype(x.dtype)


def _group_norm(o):
    mu = jnp.mean(o, axis=-1, keepdims=True)
    var = jnp.mean(jnp.square(o - mu), axis=-1, keepdims=True)
    return (o - mu) * lax.rsqrt(var + EPS)


def _rotary(x, start):
    T, dk = x.shape[1], x.shape[-1]
    angle = 1.0 / (ROPE_BASE ** jnp.linspace(0.0, 1.0, dk // 2, dtype=jnp.float32))
    angle = jnp.repeat(angle, 2)
    pos = start + jnp.arange(T, dtype=jnp.float32)
    th = pos[:, None] * angle[None, :]
    sin = jnp.sin(th)[None, :, None, :]
    cos = jnp.cos(th)[None, :, None, :]
    xf = x.astype(jnp.float32)
    rot = jnp.stack([-xf[..., 1::2], xf[..., 0::2]], axis=-1).reshape(xf.shape)
    return xf * cos + rot * sin


def _retention_scan(q, k, v, s0):
    B, T, H, dk = q.shape
    dv = v.shape[-1]
    C = RET_CHUNK if T % RET_CHUNK == 0 else T
    NC = T // C
    lg = jnp.log1p(-jnp.exp2(-5.0 - jnp.arange(H, dtype=jnp.float32)))
    idx = jnp.arange(C, dtype=jnp.float32)
    rel = idx[:, None] - idx[None, :]
    decay_in = jnp.where(rel[None] >= 0, jnp.exp(lg[:, None, None] * jnp.maximum(rel, 0.0)[None]), 0.0)
    dec_q = jnp.exp(lg[:, None] * (idx + 1.0)[None])
    dec_k = jnp.exp(lg[:, None] * (C - 1.0 - idx)[None])
    dec_c = jnp.exp(lg * C)

    def chunks(a):
        return a.astype(jnp.float32).reshape(B, NC, C, H, a.shape[-1]).transpose(1, 0, 3, 2, 4)

    def step(s, inp):
        qc, kc, vc = inp
        scores = jnp.einsum('bhid,bhjd->bhij', qc, kc) * decay_in[None]
        o = (jnp.einsum('bhij,bhje->bhie', scores, vc)
             + jnp.einsum('bhid,bhde->bhie', qc, s) * dec_q[None, :, :, None])
        s = s * dec_c[None, :, None, None] + jnp.einsum('bhjd,bhje->bhde', kc * dec_k[None, :, :, None], vc)
        return s, o

    s_fin, o = lax.scan(step, s0.astype(jnp.float32), (chunks(q), chunks(k), chunks(v)))
    o = o.transpose(1, 0, 3, 2, 4).reshape(B, T, H, dv)
    return o, s_fin


def _retention_mixer(h, s0, start, w_in, w_out):
    B, T, _ = h.shape
    proj = h @ w_in
    q = proj[..., :D_MODEL].reshape(B, T, RET_HEADS, RET_DK)
    k = proj[..., D_MODEL:2 * D_MODEL].reshape(B, T, RET_HEADS, RET_DK)
    v = proj[..., 2 * D_MODEL:2 * D_MODEL + RET_HEADS * RET_DV].reshape(B, T, RET_HEADS, RET_DV)
    g = proj[..., 2 * D_MODEL + RET_HEADS * RET_DV:]
    q = _rotary(q, start)
    k = _rotary(k, start) * (RET_DK ** -0.5)
    o, s_fin = _retention_scan(q, k, v, s0)
    o = _group_norm(o).reshape(B, T, RET_HEADS * RET_DV).astype(h.dtype)
    return (jax.nn.silu(g) * o) @ w_out, s_fin


def _diff_project(h, w_in):
    B, T, _ = h.shape
    proj = h @ w_in
    q = proj[..., :D_MODEL].reshape(B, T, DIFF_HEADS, 2, DIFF_DH)
    k = proj[..., D_MODEL:2 * D_MODEL].reshape(B, T, DIFF_HEADS, 2 * DIFF_DH)
    v = proj[..., 2 * D_MODEL:].reshape(B, T, DIFF_HEADS, DIFF_DV)
    return q, k, v


def _diff_attend_prompt(q, k, v, lam):
    B, T, H, _, dh = q.shape
    Qb = Q_BLOCK if T % Q_BLOCK == 0 else T
    NB = T // Qb
    scale = dh ** -0.5
    k1 = k[..., :dh].astype(jnp.float32)
    k2 = k[..., dh:].astype(jnp.float32)
    vf = v.astype(jnp.float32)
    kpos = jnp.arange(T, dtype=jnp.int32)

    def block(args):
        qb, start = args
        qb = qb.astype(jnp.float32)
        qpos = start + jnp.arange(Qb, dtype=jnp.int32)
        mask = (kpos[None, :] <= qpos[:, None])[None, None]
        s1 = jnp.einsum('bqhd,bkhd->bhqk', qb[..., 0, :], k1) * scale
        s2 = jnp.einsum('bqhd,bkhd->bhqk', qb[..., 1, :], k2) * scale
        a1 = jax.nn.softmax(jnp.where(mask, s1, NEG_INF), axis=-1)
        a2 = jax.nn.softmax(jnp.where(mask, s2, NEG_INF), axis=-1)
        return jnp.einsum('bhqk,bkhe->bqhe', a1 - lam * a2, vf)

    qblocks = q.reshape(B, NB, Qb, H, 2, dh).transpose(1, 0, 2, 3, 4, 5)
    starts = jnp.arange(NB, dtype=jnp.int32) * Qb
    o = lax.map(block, (qblocks, starts))
    return o.transpose(1, 0, 2, 3, 4).reshape(B, T, H, v.shape[-1])


def _diff_attend_sample(q, k_new, v_new, k_past, v_past, lam):
    T = q.shape[1]
    P = k_past.shape[1]
    dh = q.shape[-1]
    scale = dh ** -0.5
    qf = q.astype(jnp.float32)
    tpos = jnp.arange(T, dtype=jnp.int32)
    mask = (tpos[None, :] <= tpos[:, None])[None, None]

    def probs(i):
        sp = jnp.einsum('bqhd,bkhd->bhqk', qf[..., i, :], k_past[..., i * dh:(i + 1) * dh].astype(jnp.float32)) * scale
        sn = jnp.einsum('bqhd,bkhd->bhqk', qf[..., i, :], k_new[..., i * dh:(i + 1) * dh].astype(jnp.float32)) * scale
        sn = jnp.where(mask, sn, NEG_INF)
        return jax.nn.softmax(jnp.concatenate([sp, sn], axis=-1), axis=-1)

    a = probs(0) - lam * probs(1)
    return (jnp.einsum('bhqk,bkhe->bqhe', a[..., :P], v_past.astype(jnp.float32))
            + jnp.einsum('bhqk,bkhe->bqhe', a[..., P:], v_new.astype(jnp.float32)))


def _diff_out(o, subln, lam_init, w_out, dtype):
    B, T = o.shape[0], o.shape[1]
    o = _rmsnorm(o, subln) * (1.0 - lam_init)
    return o.reshape(B, T, DIFF_HEADS * DIFF_DV).astype(dtype) @ w_out


def _ffn(h, w_up, w_down):
    return jnp.square(jax.nn.relu(h @ w_up)) @ w_down


def _ple(y, p, g_norm, w_gate, w_proj):
    gate = jax.nn.sigmoid(_rmsnorm(y, g_norm) @ w_gate)
    return gate * (p @ w_proj)


def setup_inputs(seed: int = 0) -> dict:
    key = jax.random.key(seed)
    ks = jax.random.split(key, 26)
    f32 = jnp.float32
    n_pages = PAST_LEN // PAGE_SIZE
    n_used = DEC_BATCH * n_pages
    n_phys = n_used + n_used // 4

    def nrm(k, shape, scale=1.0):
        return jax.random.normal(k, shape, f32) * scale

    return {
        'x_prompt': nrm(ks[0], (BATCH, SEQ, D_MODEL)),
        'x_sample': nrm(ks[1], (DEC_BATCH, DEC_SEQ, D_MODEL)),
        'state_ret': nrm(ks[2], (N_RET_LAYERS, DEC_BATCH, RET_HEADS, RET_DK, RET_DV)),
        'cache_k': nrm(ks[3], (N_DIFF_LAYERS, n_phys, PAGE_SIZE, DIFF_HEADS, 2 * DIFF_DH)),
        'cache_v': nrm(ks[4], (N_DIFF_LAYERS, n_phys, PAGE_SIZE, DIFF_HEADS, DIFF_DV)),
        'page_table': jax.random.permutation(ks[5], n_phys)[:n_used].reshape(DEC_BATCH, n_pages).astype(jnp.int32),
        'p_prompt': nrm(ks[6], (DEPTH, BATCH, SEQ, D_PLE)),
        'p_sample': nrm(ks[7], (DEPTH, DEC_BATCH, DEC_SEQ, D_PLE)),
        'norm_mix': 1.0 + nrm(ks[8], (DEPTH, D_MODEL), 0.01),
        'ret_w_in': nrm(ks[9], (N_RET_LAYERS, D_MODEL, 2 * D_MODEL + 2 * RET_HEADS * RET_DV), D_MODEL ** -0.5),
        'ret_w_out': nrm(ks[10], (N_RET_LAYERS, RET_HEADS * RET_DV, D_MODEL), (RET_HEADS * RET_DV) ** -0.5),
        'diff_w_in': nrm(ks[11], (N_DIFF_LAYERS, D_MODEL, 3 * D_MODEL), D_MODEL ** -0.5),
        'diff_w_out': nrm(ks[12], (N_DIFF_LAYERS, DIFF_HEADS * DIFF_DV, D_MODEL), (DIFF_HEADS * DIFF_DV) ** -0.5),
        'diff_lambda_q1': nrm(ks[13], (N_DIFF_LAYERS, DIFF_DH), 0.1),
        'diff_lambda_k1': nrm(ks[14], (N_DIFF_LAYERS, DIFF_DH), 0.1),
        'diff_lambda_q2': nrm(ks[15], (N_DIFF_LAYERS, DIFF_DH), 0.1),
        'diff_lambda_k2': nrm(ks[16], (N_DIFF_LAYERS, DIFF_DH), 0.1),
        'diff_subln': 1.0 + nrm(ks[17], (N_DIFF_LAYERS, DIFF_DV), 0.01),
        'norm_ffn': 1.0 + nrm(ks[18], (DEPTH, D_MODEL), 0.01),
        'w_up': nrm(ks[19], (DEPTH, D_MODEL, D_FF), D_MODEL ** -0.5),
        'w_down': nrm(ks[20], (DEPTH, D_FF, D_MODEL), D_FF ** -0.5),
        'ple_norm': 1.0 + nrm(ks[21], (DEPTH, D_MODEL), 0.01),
        'w_ple_gate': nrm(ks[22], (DEPTH, D_MODEL, D_MODEL), D_MODEL ** -0.5),
        'w_ple_proj': nrm(ks[23], (DEPTH, D_PLE, D_MODEL), D_PLE ** -0.5),
        'final_norm': 1.0 + nrm(ks[24], (D_MODEL,), 0.01),
    }


def reference(x_prompt, x_sample, state_ret, cache_k, cache_v, page_table, p_prompt, p_sample,
              norm_mix, ret_w_in, ret_w_out, diff_w_in, diff_w_out,
              diff_lambda_q1, diff_lambda_k1, diff_lambda_q2, diff_lambda_k2, diff_subln,
              norm_ffn, w_up, w_down, ple_norm, w_ple_gate, w_ple_proj, final_norm):
    yp, ys = x_prompt, x_sample
    n_seq, n_pages = page_table.shape
    ret_p, ret_s, kp_rows, vp_rows, ks_rows, vs_rows = [], [], [], [], [], []
    for i in range(DEPTH):
        hp = _rmsnorm(yp, norm_mix[i])
        hs = _rmsnorm(ys, norm_mix[i])
        if i % N_MIXERS == 0:
            r = i // N_MIXERS
            s0 = jnp.zeros((yp.shape[0], RET_HEADS, RET_DK, RET_DV), jnp.float32)
            mp, sp = _retention_mixer(hp, s0, 0, ret_w_in[r], ret_w_out[r])
            ms, ss = _retention_mixer(hs, state_ret[r], PAST_LEN, ret_w_in[r], ret_w_out[r])
            ret_p.append(sp.astype(x_prompt.dtype))
            ret_s.append(ss.astype(state_ret.dtype))
        else:
            d = i // N_MIXERS
            lam_init = 0.8 - 0.6 * math.exp(-0.3 * i)
            lam = (jnp.exp(jnp.sum(diff_lambda_q1[d].astype(jnp.float32) * diff_lambda_k1[d].astype(jnp.float32)))
                   - jnp.exp(jnp.sum(diff_lambda_q2[d].astype(jnp.float32) * diff_lambda_k2[d].astype(jnp.float32)))
                   + lam_init)
            qp, kp, vp = _diff_project(hp, diff_w_in[d])
            op = _diff_attend_prompt(qp, kp, vp, lam)
            qs, kn, vn = _diff_project(hs, diff_w_in[d])
            k_past = cache_k[d][page_table].reshape(n_seq, n_pages * PAGE_SIZE, DIFF_HEADS, 2 * DIFF_DH)
            v_past = cache_v[d][page_table].reshape(n_seq, n_pages * PAGE_SIZE, DIFF_HEADS, DIFF_DV)
            os_ = _diff_attend_sample(qs, kn, vn, k_past, v_past, lam)
            mp = _diff_out(op, diff_subln[d], lam_init, diff_w_out[d], yp.dtype)
            ms = _diff_out(os_, diff_subln[d], lam_init, diff_w_out[d], ys.dtype)
            kp_rows.append(kp)
            vp_rows.append(vp)
            ks_rows.append(kn)
            vs_rows.append(vn)
        yp = yp + mp
        ys = ys + ms
        yp = yp + _ffn(_rmsnorm(yp, norm_ffn[i]), w_up[i], w_down[i])
        ys = ys + _ffn(_rmsnorm(ys, norm_ffn[i]), w_up[i], w_down[i])
        yp = yp + _ple(yp, p_prompt[i], ple_norm[i], w_ple_gate[i], w_ple_proj[i])
        ys = ys + _ple(ys, p_sample[i], ple_norm[i], w_ple_gate[i], w_ple_proj[i])
    y_prompt = _rmsnorm(yp, final_norm)
    y_sample = _rmsnorm(ys, final_norm)
    ret_state_prompt = jnp.stack(ret_p)
    ret_state_sample = jnp.stack(ret_s)
    k_rows_prompt = jnp.stack(kp_rows)
    v_rows_prompt = jnp.stack(vp_rows)
    k_rows_sample = jnp.stack(ks_rows)
    v_rows_sample = jnp.stack(vs_rows)
    return (y_prompt, y_sample, ret_state_prompt, ret_state_sample, k_rows_prompt, v_rows_prompt, k_rows_sample, v_rows_sample)
```

```python
import functools
import math

import jax
import jax.numpy as jnp
from jax import lax
from jax.experimental import pallas as pl
from jax.experimental.pallas import tpu as pltpu

EPS = 1e-6
NEG_INF = -1e30
N_MIXERS = 2
RET_HEADS = 4
RET_CHUNK = 128
ROPE_BASE = 10000.0
DIFF_HEADS = 8
PAST_LEN = 8192

F32 = jnp.float32
BF16 = jnp.bfloat16

VMEM_LIMIT_BYTES = 48 * 1024 * 1024
ROW_TILE = 1024
ATTN_BLOCK = 512
PAGES_PER_STEP = 8


def _cparams(*sem):
    return pltpu.CompilerParams(dimension_semantics=sem, vmem_limit_bytes=VMEM_LIMIT_BYTES)


def _row_tile(m, pref=ROW_TILE):
    return pref if m % pref == 0 else m


def _rms(x, g):
    return x * lax.rsqrt(jnp.mean(x * x, axis=-1, keepdims=True) + EPS) * g


def _dot(a, b):
    return jnp.dot(a, b, preferred_element_type=F32)


def _dot_nt(a, b):
    return lax.dot_general(a, b, (((1,), (1,)), ((), ())), preferred_element_type=F32)


def _dot_tn(a, b):
    return lax.dot_general(a, b, (((0,), (0,)), ((), ())), preferred_element_type=F32)


def _ret_proj_kernel(x_ref, g_ref, w_ref, cos_ref, sin_ref, o_ref, h_ref, *, n_qk_blocks, k_scale):
    j = pl.program_id(1)

    @pl.when(j == 0)
    def _():
        h_ref[...] = _rms(x_ref[...], g_ref[...]).astype(BF16)

    acc = _dot(h_ref[...], w_ref[...])

    @pl.when(j < n_qk_blocks)
    def _():
        tn = acc.shape[1]
        lane = lax.broadcasted_iota(jnp.int32, acc.shape, 1)
        swapped = jnp.where((lane & 1) == 0, pltpu.roll(acc, tn - 1, 1), pltpu.roll(acc, 1, 1))
        scale = jnp.where(j >= n_qk_blocks // 2, k_scale, 1.0).astype(F32)
        o_ref[...] = ((acc * cos_ref[...] + swapped * sin_ref[...]) * scale).astype(BF16)

    @pl.when(j >= n_qk_blocks)
    def _():
        o_ref[...] = acc.astype(BF16)


def _ret_proj(x, g, w, cos_t, sin_t, n_pos_blocks, k_scale):
    m, d = x.shape
    n = w.shape[1]
    tm, tn = _row_tile(m), cos_t.shape[1]
    kern = functools.partial(_ret_proj_kernel, n_qk_blocks=2 * d // tn, k_scale=k_scale)
    return pl.pallas_call(
        kern,
        grid=(m // tm, n // tn),
        in_specs=[
            pl.BlockSpec((tm, d), lambda i, j: (i, 0)),
            pl.BlockSpec((1, d), lambda i, j: (0, 0)),
            pl.BlockSpec((d, tn), lambda i, j: (0, j)),
            pl.BlockSpec((tm, tn), lambda i, j: (i % n_pos_blocks, 0)),
            pl.BlockSpec((tm, tn), lambda i, j: (i % n_pos_blocks, 0)),
        ],
        out_specs=pl.BlockSpec((tm, tn), lambda i, j: (i, j)),
        out_shape=jax.ShapeDtypeStruct((m, n), BF16),
        scratch_shapes=[pltpu.VMEM((tm, d), BF16)],
        compiler_params=_cparams("parallel", "arbitrary"),
        name="ret_proj",
    )(x, g, w, cos_t, sin_t)


def _ret_scan_kernel(decc_ref, q_ref, k_ref, v_ref, g_ref, din_ref, dq_ref, dk_ref,
                     o_ref, sfin_ref, s_ref, *, n_heads):
    c = pl.program_id(1)
    h = pl.program_id(0) % n_heads

    @pl.when(c == 0)
    def _():
        s_ref[...] = jnp.zeros_like(s_ref)

    q, k, v = q_ref[...], k_ref[...], v_ref[...]
    s_old = s_ref[...]
    scores = _dot_nt(q, k) * din_ref[0]
    o = _dot(scores.astype(BF16), v) + _dot(q, s_old.astype(BF16)) * dq_ref[0]
    k_dec = (k.astype(F32) * dk_ref[0]).astype(BF16)
    s_new = s_old * decc_ref[h] + _dot_tn(k_dec, v)
    s_ref[...] = s_new

    @pl.when(c == pl.num_programs(1) - 1)
    def _():
        sfin_ref[0, 0] = s_new

    mu = jnp.mean(o, axis=-1, keepdims=True)
    d = o - mu
    on = d * lax.rsqrt(jnp.mean(d * d, axis=-1, keepdims=True) + EPS)
    gate = g_ref[...].astype(F32)
    o_ref[...] = (gate * jax.nn.sigmoid(gate) * on).astype(BF16)


def _ret_scan(proj, batch, seq, dec_c, dec_in, dec_q, dec_k):
    n_heads = RET_HEADS
    m = proj.shape[0]
    dk = proj.shape[1] // (6 * n_heads)
    dv = 2 * dk
    c = RET_CHUNK
    nc = seq // c
    kern = functools.partial(_ret_scan_kernel, n_heads=n_heads)
    row = lambda bh, ci: (bh // n_heads) * nc + ci
    return pl.pallas_call(
        kern,
        grid=(batch * n_heads, nc),
        in_specs=[
            pl.BlockSpec(memory_space=pltpu.SMEM),
            pl.BlockSpec((c, dk), lambda bh, ci: (row(bh, ci), bh % n_heads)),
            pl.BlockSpec((c, dk), lambda bh, ci: (row(bh, ci), n_heads + bh % n_heads)),
            pl.BlockSpec((c, dv), lambda bh, ci: (row(bh, ci), n_heads + bh % n_heads)),
            pl.BlockSpec((c, dv), lambda bh, ci: (row(bh, ci), 2 * n_heads + bh % n_heads)),
            pl.BlockSpec((1, c, c), lambda bh, ci: (bh % n_heads, 0, 0)),
            pl.BlockSpec((1, c, 1), lambda bh, ci: (bh % n_heads, 0, 0)),
            pl.BlockSpec((1, c, 1), lambda bh, ci: (bh % n_heads, 0, 0)),
        ],
        out_specs=[
            pl.BlockSpec((c, dv), lambda bh, ci: (row(bh, ci), bh % n_heads)),
            pl.BlockSpec((1, 1, dk, dv), lambda bh, ci: (bh // n_heads, bh % n_heads, 0, 0)),
        ],
        out_shape=[
            jax.ShapeDtypeStruct((m, n_heads * dv), BF16),
            jax.ShapeDtypeStruct((batch, n_heads, dk, dv), F32),
        ],
        scratch_shapes=[pltpu.VMEM((dk, dv), F32)],
        compiler_params=_cparams("parallel", "arbitrary"),
        name="ret_scan",
    )(dec_c, proj, proj, proj, proj, dec_in, dec_q, dec_k)


def _ret_step_kernel(gam_ref, q_ref, k_ref, v_ref, g_ref, s_ref, o_ref, snew_ref):
    gamma = gam_ref[pl.program_id(1)]
    q, k = q_ref[0, 0], k_ref[0, 0]
    v = v_ref[0, 0]
    s_old = s_ref[0, 0]
    qk = jnp.sum(q * k, axis=0, keepdims=True)
    o = qk * v + gamma * jnp.sum(q * s_old, axis=0, keepdims=True)
    snew_ref[0, 0] = s_old * gamma + k * v
    mu = jnp.mean(o, axis=-1, keepdims=True)
    d = o - mu
    on = d * lax.rsqrt(jnp.mean(d * d, axis=-1, keepdims=True) + EPS)
    gate = g_ref[0, 0]
    o_ref[0, 0] = (gate * jax.nn.sigmoid(gate) * on).astype(BF16)


def _ret_step(q_col, k_col, v_row, g_row, state, gamma):
    nb, n_heads, dk, dv = state.shape
    idx = lambda b, h: (b, h, 0, 0)
    return pl.pallas_call(
        _ret_step_kernel,
        grid=(nb, n_heads),
        in_specs=[
            pl.BlockSpec(memory_space=pltpu.SMEM),
            pl.BlockSpec((1, 1, dk, 1), idx),
            pl.BlockSpec((1, 1, dk, 1), idx),
            pl.BlockSpec((1, 1, 1, dv), idx),
            pl.BlockSpec((1, 1, 1, dv), idx),
            pl.BlockSpec((1, 1, dk, dv), idx),
        ],
        out_specs=[
            pl.BlockSpec((1, 1, 1, dv), idx),
            pl.BlockSpec((1, 1, dk, dv), idx),
        ],
        out_shape=[
            jax.ShapeDtypeStruct((nb, n_heads, 1, dv), BF16),
            jax.ShapeDtypeStruct(state.shape, F32),
        ],
        compiler_params=_cparams("parallel", "arbitrary"),
        name="ret_step",
    )(gamma, q_col, k_col, v_row, g_row, state)


def _matmul_res_kernel(a_ref, w_ref, y_ref, o_ref):
    o_ref[...] = y_ref[...] + _dot(a_ref[...], w_ref[...])


def _matmul_res(a, w, y):
    m, k = a.shape
    n = w.shape[1]
    tm = _row_tile(m, 512)
    return pl.pallas_call(
        _matmul_res_kernel,
        grid=(m // tm,),
        in_specs=[
            pl.BlockSpec((tm, k), lambda i: (i, 0)),
            pl.BlockSpec((k, n), lambda i: (0, 0)),
            pl.BlockSpec((tm, n), lambda i: (i, 0)),
        ],
        out_specs=pl.BlockSpec((tm, n), lambda i: (i, 0)),
        out_shape=jax.ShapeDtypeStruct((m, n), F32),
        compiler_params=_cparams("parallel"),
        name="mix_out",
    )(a, w, y)


def _ffn_kernel(y_ref, g_ref, wu_ref, wd_ref, o_ref, h_ref, acc_ref):
    j = pl.program_id(1)

    @pl.when(j == 0)
    def _():
        h_ref[...] = _rms(y_ref[...], g_ref[...]).astype(BF16)
        acc_ref[...] = jnp.zeros_like(acc_ref)

    u = jnp.maximum(_dot(h_ref[...], wu_ref[...]), 0.0)
    acc_ref[...] += _dot((u * u).astype(BF16), wd_ref[...])

    @pl.when(j == pl.num_programs(1) - 1)
    def _():
        o_ref[...] = y_ref[...] + acc_ref[...]


def _ffn(y, g, w_up, w_down):
    m, d = y.shape
    f = w_up.shape[1]
    tm, tf = _row_tile(m), 512
    return pl.pallas_call(
        _ffn_kernel,
        grid=(m // tm, f // tf),
        in_specs=[
            pl.BlockSpec((tm, d), lambda i, j: (i, 0)),
            pl.BlockSpec((1, d), lambda i, j: (0, 0)),
            pl.BlockSpec((d, tf), lambda i, j: (0, j)),
            pl.BlockSpec((tf, d), lambda i, j: (j, 0)),
        ],
        out_specs=pl.BlockSpec((tm, d), lambda i, j: (i, 0)),
        out_shape=jax.ShapeDtypeStruct((m, d), F32),
        scratch_shapes=[pltpu.VMEM((tm, d), BF16), pltpu.VMEM((tm, d), F32)],
        compiler_params=_cparams("parallel", "arbitrary"),
        name="ffn",
    )(y, g, w_up, w_down)


def _ple_kernel(y_ref, p_ref, g_ref, wg_ref, wp_ref, fn_ref, o_ref, *, final_norm):
    y = y_ref[...]
    gate = jax.nn.sigmoid(_dot(_rms(y, g_ref[...]).astype(BF16), wg_ref[...]))
    out = y + gate * _dot(p_ref[...].astype(BF16), wp_ref[...])
    if final_norm:
        out = _rms(out, fn_ref[...])
    o_ref[...] = out


def _ple(y, p, g, w_gate, w_proj, fn, final_norm):
    m, d = y.shape
    dp = p.shape[1]
    tm = _row_tile(m, 512)
    kern = functools.partial(_ple_kernel, final_norm=final_norm)
    return pl.pallas_call(
        kern,
        grid=(m // tm,),
        in_specs=[
            pl.BlockSpec((tm, d), lambda i: (i, 0)),
            pl.BlockSpec((tm, dp), lambda i: (i, 0)),
            pl.BlockSpec((1, d), lambda i: (0, 0)),
            pl.BlockSpec((d, d), lambda i: (0, 0)),
            pl.BlockSpec((dp, d), lambda i: (0, 0)),
            pl.BlockSpec((1, d), lambda i: (0, 0)),
        ],
        out_specs=pl.BlockSpec((tm, d), lambda i: (i, 0)),
        out_shape=jax.ShapeDtypeStruct((m, d), F32),
        compiler_params=_cparams("parallel"),
        name="ple",
    )(y, p, g, w_gate, w_proj, fn)


def _diff_proj_kernel(x_ref, g_ref, w_ref, q_ref, kf_ref, kb_ref, vf_ref, vb_ref, h_ref, *, q_scale):
    j = pl.program_id(1)

    @pl.when(j == 0)
    def _():
        h_ref[...] = _rms(x_ref[...], g_ref[...]).astype(BF16)

    acc = _dot(h_ref[...], w_ref[...])

    @pl.when(j == 0)
    def _():
        q_ref[...] = (acc * q_scale).astype(BF16)

    @pl.when(j == 1)
    def _():
        kf_ref[...] = acc
        kb_ref[...] = acc.astype(BF16)

    @pl.when(j == 2)
    def _():
        vf_ref[...] = acc
        vb_ref[...] = acc.astype(BF16)


def _diff_proj(x, g, w, q_scale):
    m, d = x.shape
    tm = _row_tile(m, 512)
    kern = functools.partial(_diff_proj_kernel, q_scale=q_scale)
    out_spec = pl.BlockSpec((tm, d), lambda i, j: (i, 0))
    return pl.pallas_call(
        kern,
        grid=(m // tm, 3),
        in_specs=[
            pl.BlockSpec((tm, d), lambda i, j: (i, 0)),
            pl.BlockSpec((1, d), lambda i, j: (0, 0)),
            pl.BlockSpec((d, d), lambda i, j: (0, j)),
        ],
        out_specs=[out_spec] * 5,
        out_shape=[
            jax.ShapeDtypeStruct((m, d), BF16),
            jax.ShapeDtypeStruct((m, d), F32),
            jax.ShapeDtypeStruct((m, d), BF16),
            jax.ShapeDtypeStruct((m, d), F32),
            jax.ShapeDtypeStruct((m, d), BF16),
        ],
        scratch_shapes=[pltpu.VMEM((tm, d), BF16)],
        compiler_params=_cparams("parallel", "arbitrary"),
        name="diff_proj",
    )(x, g, w)


def _lambda_value(lq1, lk1, lq2, lk2, lam_init):
    s1 = jnp.sum(lq1 * lk1, axis=-1, keepdims=True)
    s2 = jnp.sum(lq2 * lk2, axis=-1, keepdims=True)
    return jnp.exp(s1) - jnp.exp(s2) + lam_init


def _subln(o, g, lam_init):
    return _rms(o, g) * (1.0 - lam_init)


def _diff_attn_kernel(qi_tab, ki_tab, q_ref, k_ref, v_ref, lq1_ref, lk1_ref, lq2_ref, lk2_ref,
                      sub_ref, o_ref, m_ref, l_ref, acc_ref, *, lam_init, dh):
    p = pl.program_id(1)
    qi, ki = qi_tab[p], ki_tab[p]
    tq, tk = q_ref.shape[0], k_ref.shape[0]

    @pl.when(ki == 0)
    def _():
        m_ref[...] = jnp.full_like(m_ref, NEG_INF)
        l_ref[...] = jnp.zeros_like(l_ref)
        acc_ref[...] = jnp.zeros_like(acc_ref)

    q, k, v = q_ref[...], k_ref[...], v_ref[...]
    row = qi * tq + lax.broadcasted_iota(jnp.int32, (tq, tk), 0)
    col = ki * tk + lax.broadcasted_iota(jnp.int32, (tq, tk), 1)
    visible = col <= row
    for i in range(2):
        s = _dot_nt(q[:, i * dh:(i + 1) * dh], k[:, i * dh:(i + 1) * dh])
        s = jnp.where(visible, s, NEG_INF)
        m_old = m_ref[i]
        m_new = jnp.maximum(m_old, jnp.max(s, axis=-1, keepdims=True))
        alpha = jnp.exp(m_old - m_new)
        pr = jnp.exp(s - m_new)
        l_ref[i] = alpha * l_ref[i] + jnp.sum(pr, axis=-1, keepdims=True)
        acc_ref[i] = alpha * acc_ref[i] + _dot(pr.astype(BF16), v)
        m_ref[i] = m_new

    @pl.when(ki == qi)
    def _():
        lam = _lambda_value(lq1_ref[...], lk1_ref[...], lq2_ref[...], lk2_ref[...], lam_init)
        o = acc_ref[0] / l_ref[0] - lam * (acc_ref[1] / l_ref[1])
        o_ref[...] = _subln(o, sub_ref[...], lam_init).astype(BF16)


def _diff_attn(q, k, v, batch, seq, lq1, lk1, lq2, lk2, subln, lam_init):
    n_heads = DIFF_HEADS
    m, d = q.shape
    dv = d // n_heads
    dh = dv // 2
    t = ATTN_BLOCK if seq % ATTN_BLOCK == 0 else seq
    nb = seq // t
    pairs = [(a, b) for a in range(nb) for b in range(a + 1)]
    qi_tab = jnp.asarray([a for a, _ in pairs], jnp.int32)
    ki_tab = jnp.asarray([b for _, b in pairs], jnp.int32)
    kern = functools.partial(_diff_attn_kernel, lam_init=lam_init, dh=dh)
    q_map = lambda bh, p, qt, kt: ((bh // n_heads) * nb + qt[p], bh % n_heads)
    kv_map = lambda bh, p, qt, kt: ((bh // n_heads) * nb + kt[p], bh % n_heads)
    vec = lambda n: pl.BlockSpec((1, n), lambda bh, p, qt, kt: (0, 0))
    return pl.pallas_call(
        kern,
        grid_spec=pltpu.PrefetchScalarGridSpec(
            num_scalar_prefetch=2,
            grid=(batch * n_heads, len(pairs)),
            in_specs=[
                pl.BlockSpec((t, dv), q_map),
                pl.BlockSpec((t, dv), kv_map),
                pl.BlockSpec((t, dv), kv_map),
                vec(dh), vec(dh), vec(dh), vec(dh), vec(dv),
            ],
            out_specs=pl.BlockSpec((t, dv), q_map),
            scratch_shapes=[
                pltpu.VMEM((2, t, 1), F32),
                pltpu.VMEM((2, t, 1), F32),
                pltpu.VMEM((2, t, dv), F32),
            ],
        ),
        out_shape=jax.ShapeDtypeStruct((m, d), BF16),
        compiler_params=_cparams("parallel", "arbitrary"),
        name="diff_attn",
    )(qi_tab, ki_tab, q, k, v, lq1, lk1, lq2, lk2, subln)


def _diff_decode_kernel(pt_ref, q_ref, kn_ref, vn_ref, lq1_ref, lk1_ref, lq2_ref, lk2_ref, sub_ref,
                        *refs, lam_init, dh, n_pages_step):
    k_refs = refs[:n_pages_step]
    v_refs = refs[n_pages_step:2 * n_pages_step]
    o_ref, m_ref, l_ref, acc_ref = refs[2 * n_pages_step:]
    s_idx = pl.program_id(1)
    n_heads = q_ref.shape[1]

    @pl.when(s_idx == 0)
    def _():
        m_ref[...] = jnp.full_like(m_ref, NEG_INF)
        l_ref[...] = jnp.zeros_like(l_ref)
        acc_ref[...] = jnp.zeros_like(acc_ref)

    q = q_ref[0]
    lane = lax.broadcasted_iota(jnp.int32, q.shape, 1)
    first = lane < dh
    qm = jnp.concatenate([jnp.where(first, q, 0.0), jnp.where(first, 0.0, q)], axis=0).astype(BF16)
    s = jnp.concatenate([_dot_nt(qm, kr[0].astype(BF16)) for kr in k_refs], axis=1)
    head_mask = n_heads - 1
    own = ((lax.broadcasted_iota(jnp.int32, s.shape, 1) & head_mask)
           == (lax.broadcasted_iota(jnp.int32, s.shape, 0) & head_mask))
    s = jnp.where(own, s, NEG_INF)
    m_old = m_ref[...]
    m_new = jnp.maximum(m_old, jnp.max(s, axis=-1, keepdims=True))
    alpha = jnp.exp(m_old - m_new)
    pr = jnp.exp(s - m_new)
    l_new = alpha * l_ref[...] + jnp.sum(pr, axis=-1, keepdims=True)
    pb = pr.astype(BF16)
    rows = k_refs[0].shape[1]
    pv = _dot(pb[:, :rows], v_refs[0][0].astype(BF16))
    for u in range(1, n_pages_step):
        pv += _dot(pb[:, u * rows:(u + 1) * rows], v_refs[u][0].astype(BF16))
    acc_new = alpha * acc_ref[...] + pv
    m_ref[...] = m_new
    l_ref[...] = l_new
    acc_ref[...] = acc_new

    @pl.when(s_idx == pl.num_programs(1) - 1)
    def _():
        kn, vn = kn_ref[0], vn_ref[0]
        prod = q * kn
        sn = jnp.concatenate([jnp.sum(jnp.where(first, prod, 0.0), axis=-1, keepdims=True),
                              jnp.sum(jnp.where(first, 0.0, prod), axis=-1, keepdims=True)], axis=0)
        m_fin = jnp.maximum(m_new, sn)
        a2 = jnp.exp(m_new - m_fin)
        pn = jnp.exp(sn - m_fin)
        l_fin = a2 * l_new + pn
        acc_fin = a2 * acc_new + pn * jnp.concatenate([vn, vn], axis=0)
        o2 = acc_fin / l_fin
        lam = _lambda_value(lq1_ref[...], lk1_ref[...], lq2_ref[...], lk2_ref[...], lam_init)
        o = o2[:n_heads] - lam * o2[n_heads:]
        o_ref[0] = _subln(o, sub_ref[...], lam_init).astype(BF16)


def _diff_decode(q, k_new, v_new, cache_k, cache_v, page_table, lq1, lk1, lq2, lk2, subln, lam_init):
    nb, n_heads, dv = q.shape
    assert n_heads & (n_heads - 1) == 0
    dh = dv // 2
    n_pages = page_table.shape[1]
    pps = PAGES_PER_STEP if n_pages % PAGES_PER_STEP == 0 else n_pages
    rows = cache_k.shape[1]
    kern = functools.partial(_diff_decode_kernel, lam_init=lam_init, dh=dh, n_pages_step=pps)
    tok = pl.BlockSpec((1, n_heads, dv), lambda b, s, pt: (b, 0, 0))
    vec = lambda n: pl.BlockSpec((1, n), lambda b, s, pt: (0, 0))
    page = lambda u: pl.BlockSpec((1, rows, dv), lambda b, s, pt: (pt[b, s * pps + u], 0, 0))
    return pl.pallas_call(
        kern,
        grid_spec=pltpu.PrefetchScalarGridSpec(
            num_scalar_prefetch=1,
            grid=(nb, n_pages // pps),
            in_specs=[tok, tok, tok, vec(dh), vec(dh), vec(dh), vec(dh), vec(dv)]
            + [page(u) for u in range(pps)] + [page(u) for u in range(pps)],
            out_specs=tok,
            scratch_shapes=[
                pltpu.VMEM((2 * n_heads, 1), F32),
                pltpu.VMEM((2 * n_heads, 1), F32),
                pltpu.VMEM((2 * n_heads, dv), F32),
            ],
        ),
        out_shape=jax.ShapeDtypeStruct((nb, n_heads, dv), BF16),
        compiler_params=_cparams("parallel", "arbitrary"),
        name="diff_decode",
    )(page_table, q, k_new, v_new, lq1, lk1, lq2, lk2, subln,
      *([cache_k] * pps), *([cache_v] * pps))


def _rotary_tables(start, length, dk, reps, rows=None):
    angle = 1.0 / (ROPE_BASE ** jnp.linspace(0.0, 1.0, dk // 2, dtype=F32))
    angle = jnp.repeat(angle, 2)
    pos = start + jnp.arange(length, dtype=F32)
    th = pos[:, None] * angle[None, :]
    sign = jnp.where(jnp.arange(dk) % 2 == 0, -1.0, 1.0).astype(F32)
    cos_t = jnp.tile(jnp.cos(th), (1, reps))
    sin_t = jnp.tile(jnp.sin(th) * sign[None, :], (1, reps))
    if rows is not None:
        cos_t = jnp.broadcast_to(cos_t, (rows, cos_t.shape[1]))
        sin_t = jnp.broadcast_to(sin_t, (rows, sin_t.shape[1]))
    return cos_t, sin_t


def _decay_tables(n_heads, c):
    lg = jnp.log1p(-jnp.exp2(-5.0 - jnp.arange(n_heads, dtype=F32)))
    idx = jnp.arange(c, dtype=F32)
    rel = idx[:, None] - idx[None, :]
    dec_in = jnp.where(rel[None] >= 0, jnp.exp(lg[:, None, None] * jnp.maximum(rel, 0.0)[None]), 0.0)
    dec_q = jnp.exp(lg[:, None] * (idx + 1.0)[None])[:, :, None]
    dec_k = jnp.exp(lg[:, None] * (c - 1.0 - idx)[None])[:, :, None]
    dec_c = jnp.exp(lg * c)
    return dec_c, dec_in, dec_q, dec_k


def kernel(x_prompt, x_sample, state_ret, cache_k, cache_v, page_table, p_prompt, p_sample, norm_mix, ret_w_in, ret_w_out, diff_w_in, diff_w_out, diff_lambda_q1, diff_lambda_k1, diff_lambda_q2, diff_lambda_k2, diff_subln, norm_ffn, w_up, w_down, ple_norm, w_ple_gate, w_ple_proj, final_norm):
    batch, seq, d = x_prompt.shape
    nb, dec_seq, _ = x_sample.shape
    assert dec_seq == 1
    depth = norm_mix.shape[0]
    mp, ms = batch * seq, nb * dec_seq
    yp = x_prompt.reshape(mp, d)
    ys = x_sample.reshape(ms, d)
    row = lambda a: a.reshape(1, -1)
    fn = row(final_norm)

    ret_p, ret_s, kp_rows, vp_rows, ks_rows, vs_rows = [], [], [], [], [], []
    for i in range(depth):
        g_mix = row(norm_mix[i])
        if i % N_MIXERS == 0:
            r = i // N_MIXERS
            n_heads = RET_HEADS
            dk = d // n_heads
            dv = 2 * dk
            w_in = ret_w_in[r].astype(BF16)
            w_out = ret_w_out[r].astype(BF16)
            tn = 2 * dk
            k_scale = dk ** -0.5
            cos_p, sin_p = _rotary_tables(0, seq, dk, tn // dk)
            tm = _row_tile(mp)
            proj_p = _ret_proj(yp, g_mix, w_in, cos_p, sin_p, seq // tm if seq % tm == 0 else 1, k_scale)
            dec_c, dec_in, dec_q, dec_k = _decay_tables(n_heads, RET_CHUNK)
            og_p, s_p = _ret_scan(proj_p, batch, seq, dec_c, dec_in, dec_q, dec_k)
            yp = _matmul_res(og_p, w_out, yp)
            ret_p.append(s_p.astype(x_prompt.dtype))
            cos_s, sin_s = _rotary_tables(PAST_LEN, 1, dk, tn // dk, rows=ms)
            proj_s = _ret_proj(ys, g_mix, w_in, cos_s, sin_s, 1, k_scale).astype(F32)
            gamma, _, _, _ = _decay_tables(n_heads, 1)
            q_col = proj_s[:, :d].reshape(ms, n_heads, dk, 1)
            k_col = proj_s[:, d:2 * d].reshape(ms, n_heads, dk, 1)
            v_row = proj_s[:, 2 * d:2 * d + n_heads * dv].reshape(ms, n_heads, 1, dv)
            g_row = proj_s[:, 2 * d + n_heads * dv:].reshape(ms, n_heads, 1, dv)
            og_s, s_s = _ret_step(q_col, k_col, v_row, g_row, state_ret[r].astype(F32), gamma)
            ys = _matmul_res(og_s.reshape(ms, n_heads * dv), w_out, ys)
            ret_s.append(s_s.astype(state_ret.dtype))
        else:
            di = i // N_MIXERS
            n_heads = DIFF_HEADS
            dv = d // n_heads
            dh = dv // 2
            lam_init = 0.8 - 0.6 * math.exp(-0.3 * i)
            w_in = diff_w_in[di].astype(BF16)
            w_out = diff_w_out[di].astype(BF16)
            lams = [row(a[di].astype(F32)) for a in (diff_lambda_q1, diff_lambda_k1, diff_lambda_q2, diff_lambda_k2)]
            sub = row(diff_subln[di].astype(F32))
            q_scale = dh ** -0.5
            q_p, kf_p, kb_p, vf_p, vb_p = _diff_proj(yp, g_mix, w_in, q_scale)
            o_p = _diff_attn(q_p, kb_p, vb_p, batch, seq, *lams, sub, lam_init)
            yp = _matmul_res(o_p, w_out, yp)
            kp_rows.append(kf_p.reshape(batch, seq, n_heads, dv))
            vp_rows.append(vf_p.reshape(batch, seq, n_heads, dv))
            q_s, kf_s, _, vf_s, _ = _diff_proj(ys, g_mix, w_in, q_scale)
            n_phys, page_size = cache_k.shape[1], cache_k.shape[2]
            ck = cache_k[di].reshape(n_phys, page_size * n_heads, dv)
            cv = cache_v[di].reshape(n_phys, page_size * n_heads, dv)
            o_s = _diff_decode(q_s.astype(F32).reshape(ms, n_heads, dv), kf_s.reshape(ms, n_heads, dv),
                               vf_s.reshape(ms, n_heads, dv), ck, cv, page_table, *lams, sub, lam_init)
            ys = _matmul_res(o_s.reshape(ms, d), w_out, ys)
            ks_rows.append(kf_s.reshape(nb, dec_seq, n_heads, dv))
            vs_rows.append(vf_s.reshape(nb, dec_seq, n_heads, dv))
        g_ffn = row(norm_ffn[i])
        wu, wd = w_up[i].astype(BF16), w_down[i].astype(BF16)
        yp = _ffn(yp, g_ffn, wu, wd)
        ys = _ffn(ys, g_ffn, wu, wd)
        g_ple = row(ple_norm[i])
        wg, wp = w_ple_gate[i].astype(BF16), w_ple_proj[i].astype(BF16)
        last = i == depth - 1
        yp = _ple(yp, p_prompt[i].reshape(mp, -1), g_ple, wg, wp, fn, last)
        ys = _ple(ys, p_sample[i].reshape(ms, -1), g_ple, wg, wp, fn, last)

    y_prompt = yp.reshape(batch, seq, d)
    y_sample = ys.reshape(nb, dec_seq, d)
    return (y_prompt, y_sample, jnp.stack(ret_p), jnp.stack(ret_s),
            jnp.stack(kp_rows), jnp.stack(vp_rows), jnp.stack(ks_rows), jnp.stack(vs_rows))
```

```python
import functools
import math

import jax
import jax.numpy as jnp
from jax import lax
from jax.experimental import pallas as pl
from jax.experimental.pallas import tpu as pltpu

EPS = 1e-6
NEG_INF = -1e30
N_MIXERS = 2
RET_HEADS = 4
RET_CHUNK = 128
ROPE_BASE = 10000.0
DIFF_HEADS = 8
PAST_LEN = 8192

F32 = jnp.float32
BF16 = jnp.bfloat16
BF16_SUBLANES = 16

VMEM_LIMIT_BYTES = 48 * 1024 * 1024
ROW_TILE = 1024
ATTN_BLOCK = 512
PAGES_PER_STEP = 8
RET_CHUNKS_PER_STEP = 4
RET_PROJ_BLOCK = 1024
FFN_BLOCK = 1024


def _cparams(*sem):
    return pltpu.CompilerParams(dimension_semantics=sem, vmem_limit_bytes=VMEM_LIMIT_BYTES)


def _row_tile(m, pref=ROW_TILE):
    return pref if m % pref == 0 else m


def _rms(x, g):
    return x * lax.rsqrt(jnp.mean(x * x, axis=-1, keepdims=True) + EPS) * g


def _dot(a, b):
    return jnp.dot(a, b, preferred_element_type=F32)


def _dot_nt(a, b):
    return lax.dot_general(a, b, (((1,), (1,)), ((), ())), preferred_element_type=F32)


def _dot_tn(a, b):
    return lax.dot_general(a, b, (((0,), (0,)), ((), ())), preferred_element_type=F32)


def _ret_proj_kernel(x_ref, g_ref, w_ref, cos_ref, sin_ref, o_ref, h_ref, *, n_qk_blocks, k_scale):
    j = pl.program_id(1)

    @pl.when(j == 0)
    def _():
        h_ref[...] = _rms(x_ref[...], g_ref[...]).astype(BF16)

    acc = _dot(h_ref[...], w_ref[...])

    @pl.when(j < n_qk_blocks)
    def _():
        dk = cos_ref.shape[1]
        cos_t, sin_t = cos_ref[...], sin_ref[...]
        lane = lax.broadcasted_iota(jnp.int32, cos_t.shape, 1)
        even = (lane & 1) == 0
        scale = jnp.where(j >= n_qk_blocks // 2, k_scale, 1.0).astype(F32)
        for hh in range(acc.shape[1] // dk):
            a = acc[:, hh * dk:(hh + 1) * dk]
            swapped = jnp.where(even, pltpu.roll(a, dk - 1, 1), pltpu.roll(a, 1, 1))
            o_ref[:, hh * dk:(hh + 1) * dk] = ((a * cos_t + swapped * sin_t) * scale).astype(BF16)

    @pl.when(j >= n_qk_blocks)
    def _():
        o_ref[...] = acc.astype(BF16)


def _ret_proj(x, g, w, cos_t, sin_t, n_pos_blocks, k_scale):
    m, d = x.shape
    n = w.shape[1]
    dk = cos_t.shape[1]
    tm = _row_tile(m)
    tn = RET_PROJ_BLOCK if d % RET_PROJ_BLOCK == 0 else dk
    kern = functools.partial(_ret_proj_kernel, n_qk_blocks=2 * d // tn, k_scale=k_scale)
    return pl.pallas_call(
        kern,
        grid=(m // tm, n // tn),
        in_specs=[
            pl.BlockSpec((tm, d), lambda i, j: (i, 0)),
            pl.BlockSpec((1, d), lambda i, j: (0, 0)),
            pl.BlockSpec((d, tn), lambda i, j: (0, j)),
            pl.BlockSpec((tm, dk), lambda i, j: (i % n_pos_blocks, 0)),
            pl.BlockSpec((tm, dk), lambda i, j: (i % n_pos_blocks, 0)),
        ],
        out_specs=pl.BlockSpec((tm, tn), lambda i, j: (i, j)),
        out_shape=jax.ShapeDtypeStruct((m, n), BF16),
        scratch_shapes=[pltpu.VMEM((tm, d), BF16)],
        compiler_params=_cparams("parallel", "arbitrary"),
        name="ret_proj",
    )(x, g, w, cos_t, sin_t)


def _ret_scan_kernel(decc_ref, q_ref, k_ref, v_ref, g_ref, din_ref, dq_ref, dk_ref,
                     o_ref, sfin_ref, s_ref, *, n_heads):
    c = pl.program_id(1)
    h = pl.program_id(0) % n_heads

    @pl.when(c == 0)
    def _():
        s_ref[...] = jnp.zeros_like(s_ref)

    chunk = din_ref.shape[1]
    for u in range(q_ref.shape[0] // chunk):
        rows = slice(u * chunk, (u + 1) * chunk)
        q, k, v = q_ref[rows, :], k_ref[rows, :], v_ref[rows, :]
        s_old = s_ref[...]
        scores = _dot_nt(q, k) * din_ref[0]
        o = _dot(scores.astype(BF16), v) + _dot(q, s_old.astype(BF16)) * dq_ref[0]
        k_dec = (k.astype(F32) * dk_ref[0]).astype(BF16)
        s_ref[...] = s_old * decc_ref[h] + _dot_tn(k_dec, v)
        mu = jnp.mean(o, axis=-1, keepdims=True)
        d = o - mu
        on = d * lax.rsqrt(jnp.mean(d * d, axis=-1, keepdims=True) + EPS)
        gate = g_ref[rows, :].astype(F32)
        o_ref[rows, :] = (gate * jax.nn.sigmoid(gate) * on).astype(BF16)

    @pl.when(c == pl.num_programs(1) - 1)
    def _():
        sfin_ref[0, 0] = s_ref[...]


def _ret_scan(proj, batch, seq, dec_c, dec_in, dec_q, dec_k):
    n_heads = RET_HEADS
    m = proj.shape[0]
    dk = proj.shape[1] // (6 * n_heads)
    dv = 2 * dk
    chunks_per_step = RET_CHUNKS_PER_STEP if (seq // RET_CHUNK) % RET_CHUNKS_PER_STEP == 0 else 1
    c = RET_CHUNK * chunks_per_step
    nc = seq // c
    kern = functools.partial(_ret_scan_kernel, n_heads=n_heads)
    row = lambda bh, ci: (bh // n_heads) * nc + ci
    return pl.pallas_call(
        kern,
        grid=(batch * n_heads, nc),
        in_specs=[
            pl.BlockSpec(memory_space=pltpu.SMEM),
            pl.BlockSpec((c, dk), lambda bh, ci: (row(bh, ci), bh % n_heads)),
            pl.BlockSpec((c, dk), lambda bh, ci: (row(bh, ci), n_heads + bh % n_heads)),
            pl.BlockSpec((c, dv), lambda bh, ci: (row(bh, ci), n_heads + bh % n_heads)),
            pl.BlockSpec((c, dv), lambda bh, ci: (row(bh, ci), 2 * n_heads + bh % n_heads)),
            pl.BlockSpec((1, RET_CHUNK, RET_CHUNK), lambda bh, ci: (bh % n_heads, 0, 0)),
            pl.BlockSpec((1, RET_CHUNK, 1), lambda bh, ci: (bh % n_heads, 0, 0)),
            pl.BlockSpec((1, RET_CHUNK, 1), lambda bh, ci: (bh % n_heads, 0, 0)),
        ],
        out_specs=[
            pl.BlockSpec((c, dv), lambda bh, ci: (row(bh, ci), bh % n_heads)),
            pl.BlockSpec((1, 1, dk, dv), lambda bh, ci: (bh // n_heads, bh % n_heads, 0, 0)),
        ],
        out_shape=[
            jax.ShapeDtypeStruct((m, n_heads * dv), BF16),
            jax.ShapeDtypeStruct((batch, n_heads, dk, dv), F32),
        ],
        scratch_shapes=[pltpu.VMEM((dk, dv), F32)],
        compiler_params=_cparams("parallel", "arbitrary"),
        name="ret_scan",
    )(dec_c, proj, proj, proj, proj, dec_in, dec_q, dec_k)


def _ret_step_kernel(gam_ref, q_ref, k_ref, v_ref, g_ref, s_ref, o_ref, snew_ref):
    for h in range(s_ref.shape[1]):
        gamma = gam_ref[h]
        q, k = q_ref[0, h], k_ref[0, h]
        v = v_ref[0, h]
        s_old = s_ref[0, h]
        qk = jnp.sum(q * k, axis=0, keepdims=True)
        o = qk * v + gamma * jnp.sum(q * s_old, axis=0, keepdims=True)
        snew_ref[0, h] = s_old * gamma + k * v
        mu = jnp.mean(o, axis=-1, keepdims=True)
        d = o - mu
        on = d * lax.rsqrt(jnp.mean(d * d, axis=-1, keepdims=True) + EPS)
        gate = g_ref[0, h]
        o_ref[0, h] = (gate * jax.nn.sigmoid(gate) * on).astype(BF16)


def _ret_step(q_col, k_col, v_row, g_row, state, gamma):
    nb, n_heads, dk, dv = state.shape
    idx = lambda b: (b, 0, 0, 0)
    return pl.pallas_call(
        _ret_step_kernel,
        grid=(nb,),
        in_specs=[
            pl.BlockSpec(memory_space=pltpu.SMEM),
            pl.BlockSpec((1, n_heads, dk, 1), idx),
            pl.BlockSpec((1, n_heads, dk, 1), idx),
            pl.BlockSpec((1, n_heads, 1, dv), idx),
            pl.BlockSpec((1, n_heads, 1, dv), idx),
            pl.BlockSpec((1, n_heads, dk, dv), idx),
        ],
        out_specs=[
            pl.BlockSpec((1, n_heads, 1, dv), idx),
            pl.BlockSpec((1, n_heads, dk, dv), idx),
        ],
        out_shape=[
            jax.ShapeDtypeStruct((nb, n_heads, 1, dv), BF16),
            jax.ShapeDtypeStruct(state.shape, F32),
        ],
        compiler_params=_cparams("parallel"),
        name="ret_step",
    )(gamma, q_col, k_col, v_row, g_row, state)


def _matmul_res_kernel(a_ref, w_ref, y_ref, o_ref):
    o_ref[...] = y_ref[...] + _dot(a_ref[...], w_ref[...])


def _matmul_res(a, w, y):
    m, k = a.shape
    n = w.shape[1]
    tm = _row_tile(m, 512)
    return pl.pallas_call(
        _matmul_res_kernel,
        grid=(m // tm,),
        in_specs=[
            pl.BlockSpec((tm, k), lambda i: (i, 0)),
            pl.BlockSpec((k, n), lambda i: (0, 0)),
            pl.BlockSpec((tm, n), lambda i: (i, 0)),
        ],
        out_specs=pl.BlockSpec((tm, n), lambda i: (i, 0)),
        out_shape=jax.ShapeDtypeStruct((m, n), F32),
        compiler_params=_cparams("parallel"),
        name="mix_out",
    )(a, w, y)


def _ffn_kernel(y_ref, g_ref, wu_ref, wd_ref, o_ref, h_ref, acc_ref):
    j = pl.program_id(1)

    @pl.when(j == 0)
    def _():
        h_ref[...] = _rms(y_ref[...], g_ref[...]).astype(BF16)
        acc_ref[...] = jnp.zeros_like(acc_ref)

    u = jnp.maximum(_dot(h_ref[...], wu_ref[...]), 0.0)
    acc_ref[...] += _dot((u * u).astype(BF16), wd_ref[...])

    @pl.when(j == pl.num_programs(1) - 1)
    def _():
        o_ref[...] = y_ref[...] + acc_ref[...]


def _ffn(y, g, w_up, w_down):
    m, d = y.shape
    f = w_up.shape[1]
    tm, tf = _row_tile(m), FFN_BLOCK
    return pl.pallas_call(
        _ffn_kernel,
        grid=(m // tm, f // tf),
        in_specs=[
            pl.BlockSpec((tm, d), lambda i, j: (i, 0)),
            pl.BlockSpec((1, d), lambda i, j: (0, 0)),
            pl.BlockSpec((d, tf), lambda i, j: (0, j)),
            pl.BlockSpec((tf, d), lambda i, j: (j, 0)),
        ],
        out_specs=pl.BlockSpec((tm, d), lambda i, j: (i, 0)),
        out_shape=jax.ShapeDtypeStruct((m, d), F32),
        scratch_shapes=[pltpu.VMEM((tm, d), BF16), pltpu.VMEM((tm, d), F32)],
        compiler_params=_cparams("parallel", "arbitrary"),
        name="ffn",
    )(y, g, w_up, w_down)


def _ple_kernel(y_ref, p_ref, g_ref, wg_ref, wp_ref, fn_ref, o_ref, *, final_norm):
    y = y_ref[...]
    gate = jax.nn.sigmoid(_dot(_rms(y, g_ref[...]).astype(BF16), wg_ref[...]))
    out = y + gate * _dot(p_ref[...].astype(BF16), wp_ref[...])
    if final_norm:
        out = _rms(out, fn_ref[...])
    o_ref[...] = out


def _ple(y, p, g, w_gate, w_proj, fn, final_norm):
    m, d = y.shape
    dp = p.shape[1]
    tm = _row_tile(m, 512)
    kern = functools.partial(_ple_kernel, final_norm=final_norm)
    return pl.pallas_call(
        kern,
        grid=(m // tm,),
        in_specs=[
            pl.BlockSpec((tm, d), lambda i: (i, 0)),
            pl.BlockSpec((tm, dp), lambda i: (i, 0)),
            pl.BlockSpec((1, d), lambda i: (0, 0)),
            pl.BlockSpec((d, d), lambda i: (0, 0)),
            pl.BlockSpec((dp, d), lambda i: (0, 0)),
            pl.BlockSpec((1, d), lambda i: (0, 0)),
        ],
        out_specs=pl.BlockSpec((tm, d), lambda i: (i, 0)),
        out_shape=jax.ShapeDtypeStruct((m, d), F32),
        compiler_params=_cparams("parallel"),
        name="ple",
    )(y, p, g, w_gate, w_proj, fn)


def _diff_proj_kernel(x_ref, g_ref, w_ref, q_ref, kf_ref, vf_ref, *rest, q_scale, with_bf16_kv):
    h = _rms(x_ref[...], g_ref[...]).astype(BF16)
    d = h.shape[1]
    q_ref[...] = (_dot(h, w_ref[:, :d]) * q_scale).astype(BF16)
    k = _dot(h, w_ref[:, d:2 * d])
    kf_ref[...] = k
    v = _dot(h, w_ref[:, 2 * d:])
    vf_ref[...] = v
    if with_bf16_kv:
        kb_ref, vt_ref = rest
        kb_ref[...] = k.astype(BF16)
        vt_ref[0] = v.T.astype(BF16)


def _diff_proj(x, g, w, q_scale, tm, with_bf16_kv):
    m, d = x.shape
    kern = functools.partial(_diff_proj_kernel, q_scale=q_scale, with_bf16_kv=with_bf16_kv)
    out_spec = pl.BlockSpec((tm, d), lambda i: (i, 0))
    out_specs = [out_spec] * 3
    out_shape = [
        jax.ShapeDtypeStruct((m, d), BF16),
        jax.ShapeDtypeStruct((m, d), F32),
        jax.ShapeDtypeStruct((m, d), F32),
    ]
    if with_bf16_kv:
        out_specs += [out_spec, pl.BlockSpec((1, d, tm), lambda i: (i, 0, 0))]
        out_shape += [jax.ShapeDtypeStruct((m, d), BF16), jax.ShapeDtypeStruct((m // tm, d, tm), BF16)]
    return pl.pallas_call(
        kern,
        grid=(m // tm,),
        in_specs=[
            pl.BlockSpec((tm, d), lambda i: (i, 0)),
            pl.BlockSpec((1, d), lambda i: (0, 0)),
            pl.BlockSpec((d, 3 * d), lambda i: (0, 0)),
        ],
        out_specs=out_specs,
        out_shape=out_shape,
        compiler_params=_cparams("parallel"),
        name="diff_proj",
    )(x, g, w)


def _lambda_value(lq1, lk1, lq2, lk2, lam_init):
    s1 = jnp.sum(lq1 * lk1, axis=-1, keepdims=True)
    s2 = jnp.sum(lq2 * lk2, axis=-1, keepdims=True)
    return jnp.exp(s1) - jnp.exp(s2) + lam_init


def _subln(o, g, lam_init):
    return _rms(o, g) * (1.0 - lam_init)


def _diff_attn_kernel(q_ref, k_ref, vt_ref, lq1_ref, lk1_ref, lq2_ref, lk2_ref, sub_ref,
                      o_ref, m_ref, acc_ref, sa_ref, sb_ref, *, lam_init, dh):
    qi = pl.program_id(1)
    tk = k_ref.shape[1]
    dv = vt_ref.shape[1]
    q = q_ref[...]
    m_ref[...] = jnp.full_like(m_ref, NEG_INF)
    acc_ref[...] = jnp.zeros_like(acc_ref)
    ones_rows = jnp.ones((acc_ref.shape[1] - dv, tk), BF16)

    def scores(ki, s_ref):
        k = k_ref[ki]
        for i in range(2):
            s_ref[i] = _dot_nt(k[:, i * dh:(i + 1) * dh], q[:, i * dh:(i + 1) * dh])

    def update(ki, s_ref, key_offset=None):
        vt = jnp.concatenate([vt_ref[ki], ones_rows], axis=0)
        for i in range(2):
            s = s_ref[i]
            if key_offset is not None:
                key = lax.broadcasted_iota(jnp.int32, s.shape, 0) + key_offset
                qry = lax.broadcasted_iota(jnp.int32, s.shape, 1)
                s = jnp.where(key <= qry, s, NEG_INF)
            m_old = m_ref[i]
            m_new = jnp.maximum(m_old, jnp.max(s, axis=0, keepdims=True))
            alpha = jnp.exp2(m_old - m_new)
            pr = jnp.exp2(s - m_new)
            acc_ref[i] = alpha * acc_ref[i] + _dot(vt, pr.astype(BF16))
            m_ref[i] = m_new

    scores(0, sa_ref)

    def body(j, carry):
        scores(2 * j + 1, sb_ref)
        update(2 * j, sa_ref)
        scores(2 * j + 2, sa_ref)
        update(2 * j + 1, sb_ref)
        return carry

    lax.fori_loop(0, qi, body, 0)
    scores(2 * qi + 1, sb_ref)
    update(2 * qi, sa_ref, 0)
    update(2 * qi + 1, sb_ref, tk)

    lam = _lambda_value(lq1_ref[...], lk1_ref[...], lq2_ref[...], lk2_ref[...], lam_init)
    a1, a2 = acc_ref[0], acc_ref[1]
    o = a1[:dv] / a1[dv:dv + 1] - lam * (a2[:dv] / a2[dv:dv + 1])
    ms = jnp.mean(o * o, axis=0, keepdims=True)
    o = o * lax.rsqrt(ms + EPS) * sub_ref[...] * (1.0 - lam_init)
    o_ref[...] = o.T.astype(BF16)


def _diff_attn(q, k, vt, batch, seq, t, lq1, lk1, lq2, lk2, subln_col, lam_init):
    n_heads = DIFF_HEADS
    m, d = q.shape
    dv = d // n_heads
    dh = dv // 2
    nb = seq // t
    tk = t // 2
    nkb = seq // tk
    assert vt.shape == (m // tk, d, tk)
    kern = functools.partial(_diff_attn_kernel, lam_init=lam_init, dh=dh)
    vec = lambda n: pl.BlockSpec((1, n), lambda bh, qi: (0, 0))
    q_map = lambda bh, qi: ((bh // n_heads) * nb + qi, bh % n_heads)
    return pl.pallas_call(
        kern,
        grid=(batch * n_heads, nb),
        in_specs=[
            pl.BlockSpec((t, dv), q_map),
            pl.BlockSpec((nkb, tk, dv), lambda bh, qi: (bh // n_heads, 0, bh % n_heads)),
            pl.BlockSpec((nkb, dv, tk), lambda bh, qi: (bh // n_heads, bh % n_heads, 0)),
            vec(dh), vec(dh), vec(dh), vec(dh),
            pl.BlockSpec((dv, 1), lambda bh, qi: (0, 0)),
        ],
        out_specs=pl.BlockSpec((t, dv), q_map),
        out_shape=jax.ShapeDtypeStruct((m, d), BF16),
        scratch_shapes=[
            pltpu.VMEM((2, 1, t), F32),
            pltpu.VMEM((2, dv + BF16_SUBLANES, t), F32),
            pltpu.VMEM((2, tk, t), F32),
            pltpu.VMEM((2, tk, t), F32),
        ],
        compiler_params=_cparams("parallel", "arbitrary"),
        name="diff_attn",
    )(q, k.reshape(m // tk, tk, d), vt, lq1, lk1, lq2, lk2, subln_col)


def _diff_decode_kernel(pt_ref, q_ref, kn_ref, vn_ref, lq1_ref, lk1_ref, lq2_ref, lk2_ref, sub_ref,
                        *refs, lam_init, dh, n_pages_step):
    k_refs = refs[:n_pages_step]
    v_refs = refs[n_pages_step:2 * n_pages_step]
    o_ref, m_ref, l_ref, acc_ref = refs[2 * n_pages_step:]
    s_idx = pl.program_id(1)
    n_heads = q_ref.shape[1]

    @pl.when(s_idx == 0)
    def _():
        m_ref[...] = jnp.full_like(m_ref, NEG_INF)
        l_ref[...] = jnp.zeros_like(l_ref)
        acc_ref[...] = jnp.zeros_like(acc_ref)

    q = q_ref[0]
    lane = lax.broadcasted_iota(jnp.int32, q.shape, 1)
    first = lane < dh
    qm = jnp.concatenate([jnp.where(first, q, 0.0), jnp.where(first, 0.0, q)], axis=0).astype(BF16)
    s = jnp.concatenate([_dot_nt(qm, kr[0].astype(BF16)) for kr in k_refs], axis=1)
    head_mask = n_heads - 1
    own = ((lax.broadcasted_iota(jnp.int32, s.shape, 1) & head_mask)
           == (lax.broadcasted_iota(jnp.int32, s.shape, 0) & head_mask))
    s = jnp.where(own, s, NEG_INF)
    m_old = m_ref[...]
    m_new = jnp.maximum(m_old, jnp.max(s, axis=-1, keepdims=True))
    alpha = jnp.exp2(m_old - m_new)
    pr = jnp.exp2(s - m_new)
    l_new = alpha * l_ref[...] + jnp.sum(pr, axis=-1, keepdims=True)
    pb = pr.astype(BF16)
    rows = k_refs[0].shape[1]
    pv = _dot(pb[:, :rows], v_refs[0][0].astype(BF16))
    for u in range(1, n_pages_step):
        pv += _dot(pb[:, u * rows:(u + 1) * rows], v_refs[u][0].astype(BF16))
    acc_new = alpha * acc_ref[...] + pv
    m_ref[...] = m_new
    l_ref[...] = l_new
    acc_ref[...] = acc_new

    @pl.when(s_idx == pl.num_programs(1) - 1)
    def _():
        kn, vn = kn_ref[0], vn_ref[0]
        prod = q * kn
        sn = jnp.concatenate([jnp.sum(jnp.where(first, prod, 0.0), axis=-1, keepdims=True),
                              jnp.sum(jnp.where(first, 0.0, prod), axis=-1, keepdims=True)], axis=0)
        m_fin = jnp.maximum(m_new, sn)
        a2 = jnp.exp2(m_new - m_fin)
        pn = jnp.exp2(sn - m_fin)
        l_fin = a2 * l_new + pn
        acc_fin = a2 * acc_new + pn * jnp.concatenate([vn, vn], axis=0)
        o2 = acc_fin / l_fin
        lam = _lambda_value(lq1_ref[...], lk1_ref[...], lq2_ref[...], lk2_ref[...], lam_init)
        o = o2[:n_heads] - lam * o2[n_heads:]
        o_ref[0] = _subln(o, sub_ref[...], lam_init).astype(BF16)


def _diff_decode(q, k_new, v_new, cache_k, cache_v, page_table, lq1, lk1, lq2, lk2, subln, lam_init):
    nb, n_heads, dv = q.shape
    assert n_heads & (n_heads - 1) == 0
    dh = dv // 2
    n_pages = page_table.shape[1]
    pps = PAGES_PER_STEP if n_pages % PAGES_PER_STEP == 0 else n_pages
    rows = cache_k.shape[1]
    kern = functools.partial(_diff_decode_kernel, lam_init=lam_init, dh=dh, n_pages_step=pps)
    tok = pl.BlockSpec((1, n_heads, dv), lambda b, s, pt: (b, 0, 0))
    vec = lambda n: pl.BlockSpec((1, n), lambda b, s, pt: (0, 0))
    page = lambda u: pl.BlockSpec((1, rows, dv), lambda b, s, pt: (pt[b, s * pps + u], 0, 0))
    return pl.pallas_call(
        kern,
        grid_spec=pltpu.PrefetchScalarGridSpec(
            num_scalar_prefetch=1,
            grid=(nb, n_pages // pps),
            in_specs=[tok, tok, tok, vec(dh), vec(dh), vec(dh), vec(dh), vec(dv)]
            + [page(u) for u in range(pps)] + [page(u) for u in range(pps)],
            out_specs=tok,
            scratch_shapes=[
                pltpu.VMEM((2 * n_heads, 1), F32),
                pltpu.VMEM((2 * n_heads, 1), F32),
                pltpu.VMEM((2 * n_heads, dv), F32),
            ],
        ),
        out_shape=jax.ShapeDtypeStruct((nb, n_heads, dv), BF16),
        compiler_params=_cparams("parallel", "arbitrary"),
        name="diff_decode",
    )(page_table, q, k_new, v_new, lq1, lk1, lq2, lk2, subln,
      *([cache_k] * pps), *([cache_v] * pps))


def _rotary_tables(start, length, dk, rows=None):
    angle = 1.0 / (ROPE_BASE ** jnp.linspace(0.0, 1.0, dk // 2, dtype=F32))
    angle = jnp.repeat(angle, 2)
    pos = start + jnp.arange(length, dtype=F32)
    th = pos[:, None] * angle[None, :]
    sign = jnp.where(jnp.arange(dk) % 2 == 0, -1.0, 1.0).astype(F32)
    cos_t = jnp.cos(th)
    sin_t = jnp.sin(th) * sign[None, :]
    if rows is not None:
        cos_t = jnp.broadcast_to(cos_t, (rows, cos_t.shape[1]))
        sin_t = jnp.broadcast_to(sin_t, (rows, sin_t.shape[1]))
    return cos_t, sin_t


def _decay_tables(n_heads, c):
    lg = jnp.log1p(-jnp.exp2(-5.0 - jnp.arange(n_heads, dtype=F32)))
    idx = jnp.arange(c, dtype=F32)
    rel = idx[:, None] - idx[None, :]
    dec_in = jnp.where(rel[None] >= 0, jnp.exp(lg[:, None, None] * jnp.maximum(rel, 0.0)[None]), 0.0)
    dec_q = jnp.exp(lg[:, None] * (idx + 1.0)[None])[:, :, None]
    dec_k = jnp.exp(lg[:, None] * (c - 1.0 - idx)[None])[:, :, None]
    dec_c = jnp.exp(lg * c)
    return dec_c, dec_in, dec_q, dec_k


def kernel(x_prompt, x_sample, state_ret, cache_k, cache_v, page_table, p_prompt, p_sample, norm_mix, ret_w_in, ret_w_out, diff_w_in, diff_w_out, diff_lambda_q1, diff_lambda_k1, diff_lambda_q2, diff_lambda_k2, diff_subln, norm_ffn, w_up, w_down, ple_norm, w_ple_gate, w_ple_proj, final_norm):
    batch, seq, d = x_prompt.shape
    nb, dec_seq, _ = x_sample.shape
    assert dec_seq == 1
    depth = norm_mix.shape[0]
    mp, ms = batch * seq, nb * dec_seq
    yp = x_prompt.reshape(mp, d)
    ys = x_sample.reshape(ms, d)
    row = lambda a: a.reshape(1, -1)
    fn = row(final_norm)

    ret_p, ret_s, kp_rows, vp_rows, ks_rows, vs_rows = [], [], [], [], [], []
    for i in range(depth):
        g_mix = row(norm_mix[i])
        if i % N_MIXERS == 0:
            r = i // N_MIXERS
            n_heads = RET_HEADS
            dk = d // n_heads
            dv = 2 * dk
            w_in = ret_w_in[r].astype(BF16)
            w_out = ret_w_out[r].astype(BF16)
            k_scale = dk ** -0.5
            cos_p, sin_p = _rotary_tables(0, seq, dk)
            tm = _row_tile(mp)
            proj_p = _ret_proj(yp, g_mix, w_in, cos_p, sin_p, seq // tm if seq % tm == 0 else 1, k_scale)
            dec_c, dec_in, dec_q, dec_k = _decay_tables(n_heads, RET_CHUNK)
            og_p, s_p = _ret_scan(proj_p, batch, seq, dec_c, dec_in, dec_q, dec_k)
            yp = _matmul_res(og_p, w_out, yp)
            ret_p.append(s_p.astype(x_prompt.dtype))
            cos_s, sin_s = _rotary_tables(PAST_LEN, 1, dk, rows=ms)
            proj_s = _ret_proj(ys, g_mix, w_in, cos_s, sin_s, 1, k_scale).astype(F32)
            gamma, _, _, _ = _decay_tables(n_heads, 1)
            q_col = proj_s[:, :d].reshape(ms, n_heads, dk, 1)
            k_col = proj_s[:, d:2 * d].reshape(ms, n_heads, dk, 1)
            v_row = proj_s[:, 2 * d:2 * d + n_heads * dv].reshape(ms, n_heads, 1, dv)
            g_row = proj_s[:, 2 * d + n_heads * dv:].reshape(ms, n_heads, 1, dv)
            og_s, s_s = _ret_step(q_col, k_col, v_row, g_row, state_ret[r].astype(F32), gamma)
            ys = _matmul_res(og_s.reshape(ms, n_heads * dv), w_out, ys)
            ret_s.append(s_s.astype(state_ret.dtype))
        else:
            di = i // N_MIXERS
            n_heads = DIFF_HEADS
            dv = d // n_heads
            dh = dv // 2
            lam_init = 0.8 - 0.6 * math.exp(-0.3 * i)
            w_in = diff_w_in[di].astype(BF16)
            w_out = diff_w_out[di].astype(BF16)
            lams = [row(a[di].astype(F32)) for a in (diff_lambda_q1, diff_lambda_k1, diff_lambda_q2, diff_lambda_k2)]
            sub = row(diff_subln[di].astype(F32))
            q_scale = dh ** -0.5 * math.log2(math.e)
            t = ATTN_BLOCK if seq % ATTN_BLOCK == 0 else seq
            q_p, kf_p, vf_p, kb_p, vt_p = _diff_proj(yp, g_mix, w_in, q_scale, t // 2, True)
            o_p = _diff_attn(q_p, kb_p, vt_p, batch, seq, t, *lams, sub.reshape(dv, 1), lam_init)
            yp = _matmul_res(o_p, w_out, yp)
            kp_rows.append(kf_p.reshape(batch, seq, n_heads, dv))
            vp_rows.append(vf_p.reshape(batch, seq, n_heads, dv))
            q_s, kf_s, vf_s = _diff_proj(ys, g_mix, w_in, q_scale, ms, False)
            n_phys, page_size = cache_k.shape[1], cache_k.shape[2]
            ck = cache_k[di].reshape(n_phys, page_size * n_heads, dv)
            cv = cache_v[di].reshape(n_phys, page_size * n_heads, dv)
            o_s = _diff_decode(q_s.astype(F32).reshape(ms, n_heads, dv), kf_s.reshape(ms, n_heads, dv),
                               vf_s.reshape(ms, n_heads, dv), ck, cv, page_table, *lams, sub, lam_init)
            ys = _matmul_res(o_s.reshape(ms, d), w_out, ys)
            ks_rows.append(kf_s.reshape(nb, dec_seq, n_heads, dv))
            vs_rows.append(vf_s.reshape(nb, dec_seq, n_heads, dv))
        g_ffn = row(norm_ffn[i])
        wu, wd = w_up[i].astype(BF16), w_down[i].astype(BF16)
        yp = _ffn(yp, g_ffn, wu, wd)
        ys = _ffn(ys, g_ffn, wu, wd)
        g_ple = row(ple_norm[i])
        wg, wp = w_ple_gate[i].astype(BF16), w_ple_proj[i].astype(BF16)
        last = i == depth - 1
        yp = _ple(yp, p_prompt[i].reshape(mp, -1), g_ple, wg, wp, fn, last)
        ys = _ple(ys, p_sample[i].reshape(ms, -1), g_ple, wg, wp, fn, last)

    y_prompt = yp.reshape(batch, seq, d)
    y_sample = ys.reshape(nb, dec_seq, d)
    return (y_prompt, y_sample, jnp.stack(ret_p), jnp.stack(ret_s),
            jnp.stack(kp_rows), jnp.stack(vp_rows), jnp.stack(ks_rows), jnp.stack(vs_rows))
```

```python
import functools
import math

import jax
import jax.numpy as jnp
from jax import lax
from jax.experimental import pallas as pl
from jax.experimental.pallas import tpu as pltpu

EPS = 1e-6
NEG_INF = -1e30
N_MIXERS = 2
RET_HEADS = 4
RET_CHUNK = 256
ROPE_BASE = 10000.0
DIFF_HEADS = 8
PAST_LEN = 8192

F32 = jnp.float32
BF16 = jnp.bfloat16
BF16_SUBLANES = 16
LANES = 128

VMEM_LIMIT_BYTES = 48 * 1024 * 1024
ROW_TILE = 1024
ATTN_BLOCK = 512
PAGES_PER_STEP = 8
RET_CHUNKS_PER_STEP = 2
RET_PROJ_BLOCK = 1024
FFN_BLOCK = 1024


def _cparams(*sem):
    return pltpu.CompilerParams(dimension_semantics=sem, vmem_limit_bytes=VMEM_LIMIT_BYTES)


def _row_tile(m, pref=None):
    pref = ROW_TILE if pref is None else pref
    return pref if m % pref == 0 else m


def _rms(x, g):
    return x * lax.rsqrt(jnp.mean(x * x, axis=-1, keepdims=True) + EPS) * g


def _dot(a, b):
    return jnp.dot(a, b, preferred_element_type=F32)


def _dot_nt(a, b):
    return lax.dot_general(a, b, (((1,), (1,)), ((), ())), preferred_element_type=F32)


def _dot_tn(a, b):
    return lax.dot_general(a, b, (((0,), (0,)), ((), ())), preferred_element_type=F32)


def _ret_proj_kernel(x_ref, g_ref, w_ref, cos_ref, sin_ref, o_ref, h_ref, *, n_qk_blocks, k_scale):
    j = pl.program_id(1)

    @pl.when(j == 0)
    def _():
        h_ref[...] = _rms(x_ref[...], g_ref[...]).astype(BF16)

    @pl.when(j < n_qk_blocks)
    def _():
        acc = _dot(h_ref[...], w_ref[...])
        dk = cos_ref.shape[1]
        cos_t, sin_t = cos_ref[...], sin_ref[...]
        lane = lax.broadcasted_iota(jnp.int32, cos_t.shape, 1)
        even = (lane & 1) == 0
        scale = jnp.where(j >= n_qk_blocks // 2, k_scale, 1.0).astype(F32)
        for hh in range(acc.shape[1] // dk):
            a = acc[:, hh * dk:(hh + 1) * dk]
            swapped = jnp.where(even, pltpu.roll(a, dk - 1, 1), pltpu.roll(a, 1, 1))
            o_ref[:, hh * dk:(hh + 1) * dk] = ((a * cos_t + swapped * sin_t) * scale).astype(BF16)

    @pl.when(j >= n_qk_blocks)
    def _():
        o_ref[...] = _dot(h_ref[...], w_ref[...]).astype(BF16)


def _ret_proj(x, g, w, cos_t, sin_t, k_scale):
    m, d = x.shape
    n = w.shape[1]
    rows_per_seq, dk = cos_t.shape
    tm = _row_tile(rows_per_seq)
    n_pos_blocks = rows_per_seq // tm
    tn = RET_PROJ_BLOCK if d % RET_PROJ_BLOCK == 0 else dk
    kern = functools.partial(_ret_proj_kernel, n_qk_blocks=2 * d // tn, k_scale=k_scale)
    return pl.pallas_call(
        kern,
        grid=(m // tm, n // tn),
        in_specs=[
            pl.BlockSpec((tm, d), lambda i, j: (i, 0)),
            pl.BlockSpec((1, d), lambda i, j: (0, 0)),
            pl.BlockSpec((d, tn), lambda i, j: (0, j)),
            pl.BlockSpec((tm, dk), lambda i, j: (i % n_pos_blocks, 0)),
            pl.BlockSpec((tm, dk), lambda i, j: (i % n_pos_blocks, 0)),
        ],
        out_specs=pl.BlockSpec((tm, tn), lambda i, j: (i, j)),
        out_shape=jax.ShapeDtypeStruct((m, n), BF16),
        scratch_shapes=[pltpu.VMEM((tm, d), BF16)],
        compiler_params=_cparams("parallel", "arbitrary"),
        name="ret_proj",
    )(x, g, w, cos_t, sin_t)


def _ret_scan_kernel(decc_ref, q_ref, k_ref, v_ref, g_ref, din_ref, dq_ref, dk_ref,
                     o_ref, sfin_ref, s_ref, *, n_heads):
    c = pl.program_id(1)
    h = pl.program_id(0) % n_heads

    @pl.when(c == 0)
    def _():
        s_ref[...] = jnp.zeros_like(s_ref)

    chunk = din_ref.shape[1]
    for u in range(q_ref.shape[0] // chunk):
        rows = slice(u * chunk, (u + 1) * chunk)
        q, k, v = q_ref[rows, :], k_ref[rows, :], v_ref[rows, :]
        s_old = s_ref[...]
        scores = _dot_nt(q, k) * din_ref[0]
        o = _dot(scores.astype(BF16), v) + _dot(q, s_old.astype(BF16)) * dq_ref[0]
        k_dec = (k.astype(F32) * dk_ref[0]).astype(BF16)
        s_ref[...] = s_old * decc_ref[h] + _dot_tn(k_dec, v)
        mu = jnp.mean(o, axis=-1, keepdims=True)
        d = o - mu
        on = d * lax.rsqrt(jnp.mean(d * d, axis=-1, keepdims=True) + EPS)
        gate = g_ref[rows, :].astype(F32)
        o_ref[rows, :] = (gate * jax.nn.sigmoid(gate) * on).astype(BF16)

    @pl.when(c == pl.num_programs(1) - 1)
    def _():
        sfin_ref[0, 0] = s_ref[...]


def _ret_scan(proj, batch, seq, dec_c, dec_in, dec_q, dec_k):
    n_heads = RET_HEADS
    m = proj.shape[0]
    dk = proj.shape[1] // (6 * n_heads)
    dv = 2 * dk
    assert seq % RET_CHUNK == 0
    chunks_per_step = RET_CHUNKS_PER_STEP if (seq // RET_CHUNK) % RET_CHUNKS_PER_STEP == 0 else 1
    c = RET_CHUNK * chunks_per_step
    nc = seq // c
    kern = functools.partial(_ret_scan_kernel, n_heads=n_heads)
    row = lambda bh, ci: (bh // n_heads) * nc + ci
    return pl.pallas_call(
        kern,
        grid=(batch * n_heads, nc),
        in_specs=[
            pl.BlockSpec(memory_space=pltpu.SMEM),
            pl.BlockSpec((c, dk), lambda bh, ci: (row(bh, ci), bh % n_heads)),
            pl.BlockSpec((c, dk), lambda bh, ci: (row(bh, ci), n_heads + bh % n_heads)),
            pl.BlockSpec((c, dv), lambda bh, ci: (row(bh, ci), n_heads + bh % n_heads)),
            pl.BlockSpec((c, dv), lambda bh, ci: (row(bh, ci), 2 * n_heads + bh % n_heads)),
            pl.BlockSpec((1, RET_CHUNK, RET_CHUNK), lambda bh, ci: (bh % n_heads, 0, 0)),
            pl.BlockSpec((1, RET_CHUNK, 1), lambda bh, ci: (bh % n_heads, 0, 0)),
            pl.BlockSpec((1, RET_CHUNK, 1), lambda bh, ci: (bh % n_heads, 0, 0)),
        ],
        out_specs=[
            pl.BlockSpec((c, dv), lambda bh, ci: (row(bh, ci), bh % n_heads)),
            pl.BlockSpec((1, 1, dk, dv), lambda bh, ci: (bh // n_heads, bh % n_heads, 0, 0)),
        ],
        out_shape=[
            jax.ShapeDtypeStruct((m, n_heads * dv), BF16),
            jax.ShapeDtypeStruct((batch, n_heads, dk, dv), F32),
        ],
        scratch_shapes=[pltpu.VMEM((dk, dv), F32)],
        compiler_params=_cparams("parallel", "arbitrary"),
        name="ret_scan",
    )(dec_c, proj, proj, proj, proj, dec_in, dec_q, dec_k)


def _ret_step_kernel(gam_ref, q_ref, k_ref, v_ref, g_ref, s_ref, o_ref, snew_ref):
    reps = s_ref.shape[3] // q_ref.shape[3]
    for h in range(s_ref.shape[1]):
        gamma = gam_ref[h]
        q_l, k_l = q_ref[0, h], k_ref[0, h]
        q = jnp.concatenate([q_l] * reps, axis=1)
        k = jnp.concatenate([k_l] * reps, axis=1)
        v = v_ref[0, h]
        s_old = s_ref[0, h]
        qk = jnp.sum(q_l[:, :1] * k_l[:, :1], axis=0, keepdims=True)
        o = qk * v + gamma * jnp.sum(q * s_old, axis=0, keepdims=True)
        snew_ref[0, h] = s_old * gamma + k * v
        mu = jnp.mean(o, axis=-1, keepdims=True)
        d = o - mu
        on = d * lax.rsqrt(jnp.mean(d * d, axis=-1, keepdims=True) + EPS)
        gate = g_ref[0, h]
        o_ref[0, h] = (gate * jax.nn.sigmoid(gate) * on).astype(BF16)


def _ret_step(q_col, k_col, v_row, g_row, state, gamma):
    nb, n_heads, dk, dv = state.shape
    idx = lambda b: (b, 0, 0, 0)
    return pl.pallas_call(
        _ret_step_kernel,
        grid=(nb,),
        in_specs=[
            pl.BlockSpec(memory_space=pltpu.SMEM),
            pl.BlockSpec((1, n_heads, dk, LANES), idx),
            pl.BlockSpec((1, n_heads, dk, LANES), idx),
            pl.BlockSpec((1, n_heads, 1, dv), idx),
            pl.BlockSpec((1, n_heads, 1, dv), idx),
            pl.BlockSpec((1, n_heads, dk, dv), idx),
        ],
        out_specs=[
            pl.BlockSpec((1, n_heads, 1, dv), idx),
            pl.BlockSpec((1, n_heads, dk, dv), idx),
        ],
        out_shape=[
            jax.ShapeDtypeStruct((nb, n_heads, 1, dv), BF16),
            jax.ShapeDtypeStruct(state.shape, F32),
        ],
        compiler_params=_cparams("parallel"),
        name="ret_step",
    )(gamma, q_col, k_col, v_row, g_row, state)


def _matmul_res_kernel(a_ref, w_ref, y_ref, o_ref):
    o_ref[...] = y_ref[...] + _dot(a_ref[...], w_ref[...])


def _matmul_res(a, w, y):
    m, k = a.shape
    n = w.shape[1]
    tm = _row_tile(m, 512)
    return pl.pallas_call(
        _matmul_res_kernel,
        grid=(m // tm,),
        in_specs=[
            pl.BlockSpec((tm, k), lambda i: (i, 0)),
            pl.BlockSpec((k, n), lambda i: (0, 0)),
            pl.BlockSpec((tm, n), lambda i: (i, 0)),
        ],
        out_specs=pl.BlockSpec((tm, n), lambda i: (i, 0)),
        out_shape=jax.ShapeDtypeStruct((m, n), F32),
        compiler_params=_cparams("parallel"),
        name="mix_out",
    )(a, w, y)


def _ffn_kernel(y_ref, g_ref, wu_ref, wd_ref, o_ref, h_ref, acc_ref):
    j = pl.program_id(1)

    @pl.when(j == 0)
    def _():
        h_ref[...] = _rms(y_ref[...], g_ref[...]).astype(BF16)
        acc_ref[...] = jnp.zeros_like(acc_ref)

    u = jnp.maximum(_dot(h_ref[...], wu_ref[...]), 0.0)
    acc_ref[...] += _dot((u * u).astype(BF16), wd_ref[...])

    @pl.when(j == pl.num_programs(1) - 1)
    def _():
        o_ref[...] = y_ref[...] + acc_ref[...]


def _ffn(y, g, w_up, w_down):
    m, d = y.shape
    f = w_up.shape[1]
    tm, tf = _row_tile(m), FFN_BLOCK
    return pl.pallas_call(
        _ffn_kernel,
        grid=(m // tm, f // tf),
        in_specs=[
            pl.BlockSpec((tm, d), lambda i, j: (i, 0)),
            pl.BlockSpec((1, d), lambda i, j: (0, 0)),
            pl.BlockSpec((d, tf), lambda i, j: (0, j)),
            pl.BlockSpec((tf, d), lambda i, j: (j, 0)),
        ],
        out_specs=pl.BlockSpec((tm, d), lambda i, j: (i, 0)),
        out_shape=jax.ShapeDtypeStruct((m, d), F32),
        scratch_shapes=[pltpu.VMEM((tm, d), BF16), pltpu.VMEM((tm, d), F32)],
        compiler_params=_cparams("parallel", "arbitrary"),
        name="ffn",
    )(y, g, w_up, w_down)


def _ple_kernel(y_ref, p_ref, g_ref, wg_ref, wp_ref, fn_ref, o_ref, *, final_norm):
    y = y_ref[...]
    gate = jax.nn.sigmoid(_dot(_rms(y, g_ref[...]).astype(BF16), wg_ref[...]))
    out = y + gate * _dot(p_ref[...].astype(BF16), wp_ref[...])
    if final_norm:
        out = _rms(out, fn_ref[...])
    o_ref[...] = out


def _ple(y, p, g, w_gate, w_proj, fn, final_norm):
    m, d = y.shape
    dp = p.shape[1]
    tm = _row_tile(m, 512)
    kern = functools.partial(_ple_kernel, final_norm=final_norm)
    return pl.pallas_call(
        kern,
        grid=(m // tm,),
        in_specs=[
            pl.BlockSpec((tm, d), lambda i: (i, 0)),
            pl.BlockSpec((tm, dp), lambda i: (i, 0)),
            pl.BlockSpec((1, d), lambda i: (0, 0)),
            pl.BlockSpec((d, d), lambda i: (0, 0)),
            pl.BlockSpec((dp, d), lambda i: (0, 0)),
            pl.BlockSpec((1, d), lambda i: (0, 0)),
        ],
        out_specs=pl.BlockSpec((tm, d), lambda i: (i, 0)),
        out_shape=jax.ShapeDtypeStruct((m, d), F32),
        compiler_params=_cparams("parallel"),
        name="ple",
    )(y, p, g, w_gate, w_proj, fn)


def _diff_proj_kernel(x_ref, g_ref, w_ref, q_ref, kf_ref, vf_ref, *rest, q_scale, with_bf16_kv):
    h = _rms(x_ref[...], g_ref[...]).astype(BF16)
    d = h.shape[1]
    q_ref[...] = (_dot(h, w_ref[:, :d]) * q_scale).astype(BF16)
    k = _dot(h, w_ref[:, d:2 * d])
    kf_ref[...] = k
    v = _dot(h, w_ref[:, 2 * d:])
    vf_ref[...] = v
    if with_bf16_kv:
        kb_ref, vt_ref = rest
        kb_ref[...] = k.astype(BF16)
        vt_ref[0] = v.T.astype(BF16)


def _diff_proj(x, g, w, q_scale, tm, with_bf16_kv):
    m, d = x.shape
    kern = functools.partial(_diff_proj_kernel, q_scale=q_scale, with_bf16_kv=with_bf16_kv)
    out_spec = pl.BlockSpec((tm, d), lambda i: (i, 0))
    out_specs = [out_spec] * 3
    out_shape = [
        jax.ShapeDtypeStruct((m, d), BF16),
        jax.ShapeDtypeStruct((m, d), F32),
        jax.ShapeDtypeStruct((m, d), F32),
    ]
    if with_bf16_kv:
        out_specs += [out_spec, pl.BlockSpec((1, d, tm), lambda i: (i, 0, 0))]
        out_shape += [jax.ShapeDtypeStruct((m, d), BF16), jax.ShapeDtypeStruct((m // tm, d, tm), BF16)]
    return pl.pallas_call(
        kern,
        grid=(m // tm,),
        in_specs=[
            pl.BlockSpec((tm, d), lambda i: (i, 0)),
            pl.BlockSpec((1, d), lambda i: (0, 0)),
            pl.BlockSpec((d, 3 * d), lambda i: (0, 0)),
        ],
        out_specs=out_specs,
        out_shape=out_shape,
        compiler_params=_cparams("parallel"),
        name="diff_proj",
    )(x, g, w)


def _lambda_value(lq1, lk1, lq2, lk2, lam_init):
    s1 = jnp.sum(lq1 * lk1, axis=-1, keepdims=True)
    s2 = jnp.sum(lq2 * lk2, axis=-1, keepdims=True)
    return jnp.exp(s1) - jnp.exp(s2) + lam_init


def _subln(o, g, lam_init):
    return _rms(o, g) * (1.0 - lam_init)


def _diff_attn_kernel(pt_ref, q_ref, k_ref, vt_ref, lq1_ref, lk1_ref, lq2_ref, lk2_ref, sub_ref,
                      subrow_ref, qs_ref, kn_ref, vn_ref, *refs, lam_init, dh, n_pages_step, n_groups,
                      n_decode_steps, n_attn_steps, n_q_blocks):
    k_pages = refs[:n_pages_step]
    v_pages = refs[n_pages_step:2 * n_pages_step]
    o_ref, os_ref, m_ref, acc_ref, sa_ref, sb_ref, dm_ref, dl_ref, dacc_ref = refs[2 * n_pages_step:]
    qi = pl.program_id(1)
    step = pl.program_id(0) * n_q_blocks + qi

    @pl.when(step == 0)
    def _():
        dm_ref[...] = jnp.full_like(dm_ref, NEG_INF)
        dl_ref[...] = jnp.zeros_like(dl_ref)
        dacc_ref[...] = jnp.zeros_like(dacc_ref)

    tk = k_ref.shape[1]
    dv = vt_ref.shape[1]
    q = q_ref[...]
    m_ref[...] = jnp.full_like(m_ref, NEG_INF)
    acc_ref[...] = jnp.zeros_like(acc_ref)
    ones_rows = jnp.ones((acc_ref.shape[1] - dv, tk), BF16)

    def scores(ki, s_ref):
        k = k_ref[ki]
        for i in range(2):
            s_ref[i] = _dot_nt(k[:, i * dh:(i + 1) * dh], q[:, i * dh:(i + 1) * dh])

    def update(ki, s_ref, key_offset=None):
        vt = jnp.concatenate([vt_ref[ki], ones_rows], axis=0)
        for i in range(2):
            s = s_ref[i]
            if key_offset is not None:
                key = lax.broadcasted_iota(jnp.int32, s.shape, 0) + key_offset
                qry = lax.broadcasted_iota(jnp.int32, s.shape, 1)
                s = jnp.where(key <= qry, s, NEG_INF)
            m_old = m_ref[i]
            m_new = jnp.maximum(m_old, jnp.max(s, axis=0, keepdims=True))
            alpha = jnp.exp2(m_old - m_new)
            pr = jnp.exp2(s - m_new)
            acc_ref[i] = alpha * acc_ref[i] + _dot(vt, pr.astype(BF16))
            m_ref[i] = m_new

    scores(0, sa_ref)

    def body(j, carry):
        scores(2 * j + 1, sb_ref)
        update(2 * j, sa_ref)
        scores(2 * j + 2, sa_ref)
        update(2 * j + 1, sb_ref)
        return carry

    lax.fori_loop(0, qi, body, 0)
    scores(2 * qi + 1, sb_ref)
    update(2 * qi, sa_ref, 0)
    update(2 * qi + 1, sb_ref, tk)

    lam = _lambda_value(lq1_ref[...], lk1_ref[...], lq2_ref[...], lk2_ref[...], lam_init)
    a1, a2 = acc_ref[0], acc_ref[1]
    o = a1[:dv] / a1[dv:dv + 1] - lam * (a2[:dv] / a2[dv:dv + 1])
    ms = jnp.mean(o * o, axis=0, keepdims=True)
    o = o * lax.rsqrt(ms + EPS) * sub_ref[...] * (1.0 - lam_init)
    o_ref[...] = o.T.astype(BF16)

    def decode():
        _decode_update(step % n_groups == 0, lam, qs_ref, kn_ref, vn_ref, subrow_ref, k_pages, v_pages,
                       os_ref, dm_ref, dl_ref, dacc_ref, lam_init=lam_init, dh=dh)

    if n_decode_steps == n_attn_steps:
        decode()
    else:
        pl.when(step < n_decode_steps)(decode)


def _diff_attn(q, k, vt, batch, seq, t, q_dec, k_new, v_new, cache_k, cache_v, page_table,
               lq1, lk1, lq2, lk2, subln_row, lam_init):
    n_heads = DIFF_HEADS
    assert n_heads & (n_heads - 1) == 0
    m, d = q.shape
    dv = d // n_heads
    dh = dv // 2
    nb = seq // t
    tk = t // 2
    nkb = seq // tk
    assert vt.shape == (m // tk, d, tk)
    n_attn_steps = batch * n_heads * nb
    n_dec, n_pages = page_table.shape
    pps = max(p for p in range(1, PAGES_PER_STEP + 1) if n_pages % p == 0)
    n_groups = n_pages // pps
    n_decode_steps = n_dec * n_groups
    assert n_decode_steps <= n_attn_steps, "decode page groups must fit in the prompt attention grid"
    rows = cache_k.shape[1]
    kern = functools.partial(_diff_attn_kernel, lam_init=lam_init, dh=dh, n_pages_step=pps, n_groups=n_groups,
                             n_decode_steps=n_decode_steps, n_attn_steps=n_attn_steps, n_q_blocks=nb)
    vec = lambda n: pl.BlockSpec((1, n), lambda bh, qi, pt: (0, 0))
    q_map = lambda bh, qi, pt: ((bh // n_heads) * nb + qi, bh % n_heads)
    dstep = lambda bh, qi: jnp.minimum(bh * nb + qi, n_decode_steps - 1)
    tok = pl.BlockSpec((1, n_heads, dv), lambda bh, qi, pt: (dstep(bh, qi) // n_groups, 0, 0))

    def page(u):
        def index(bh, qi, pt):
            s = dstep(bh, qi)
            return (pt[s // n_groups, (s % n_groups) * pps + u], 0, 0)
        return pl.BlockSpec((1, rows, dv), index)

    return pl.pallas_call(
        kern,
        grid_spec=pltpu.PrefetchScalarGridSpec(
            num_scalar_prefetch=1,
            grid=(batch * n_heads, nb),
            in_specs=[
                pl.BlockSpec((t, dv), q_map),
                pl.BlockSpec((nkb, tk, dv), lambda bh, qi, pt: (bh // n_heads, 0, bh % n_heads)),
                pl.BlockSpec((nkb, dv, tk), lambda bh, qi, pt: (bh // n_heads, bh % n_heads, 0)),
                vec(dh), vec(dh), vec(dh), vec(dh),
                pl.BlockSpec((dv, 1), lambda bh, qi, pt: (0, 0)),
                vec(dv), tok, tok, tok,
            ] + [page(u) for u in range(pps)] + [page(u) for u in range(pps)],
            out_specs=[pl.BlockSpec((t, dv), q_map), tok],
            scratch_shapes=[
                pltpu.VMEM((2, 1, t), F32),
                pltpu.VMEM((2, dv + BF16_SUBLANES, t), F32),
                pltpu.VMEM((2, tk, t), F32),
                pltpu.VMEM((2, tk, t), F32),
                pltpu.VMEM((2 * n_heads, 1), F32),
                pltpu.VMEM((2 * n_heads, 1), F32),
                pltpu.VMEM((2 * n_heads, dv), F32),
            ],
        ),
        out_shape=[jax.ShapeDtypeStruct((m, d), BF16), jax.ShapeDtypeStruct((n_dec, n_heads, dv), BF16)],
        compiler_params=_cparams("arbitrary", "arbitrary"),
        name="diff_attn",
    )(page_table, q, k.reshape(m // tk, tk, d), vt, lq1, lk1, lq2, lk2, subln_row.reshape(dv, 1),
      subln_row, q_dec, k_new, v_new, *([cache_k] * pps), *([cache_v] * pps))


def _decode_update(first_group, lam, q_ref, kn_ref, vn_ref, sub_ref, k_refs, v_refs,
                   o_ref, m_ref, l_ref, acc_ref, *, lam_init, dh):
    n_heads = q_ref.shape[1]
    n_pages_step = len(k_refs)
    q = q_ref[0]
    lane = lax.broadcasted_iota(jnp.int32, q.shape, 1)
    first = lane < dh
    qm = jnp.concatenate([jnp.where(first, q, 0.0), jnp.where(first, 0.0, q)], axis=0).astype(BF16)
    s = jnp.concatenate([_dot_nt(qm, kr[0].astype(BF16)) for kr in k_refs], axis=1)
    head_mask = n_heads - 1
    own = ((lax.broadcasted_iota(jnp.int32, s.shape, 1) & head_mask)
           == (lax.broadcasted_iota(jnp.int32, s.shape, 0) & head_mask))
    s = jnp.where(own, s, NEG_INF)
    m_old = jnp.where(first_group, NEG_INF, m_ref[...])
    l_old = jnp.where(first_group, 0.0, l_ref[...])
    acc_old = jnp.where(first_group, 0.0, acc_ref[...])
    m_new = jnp.maximum(m_old, jnp.max(s, axis=-1, keepdims=True))
    alpha = jnp.exp2(m_old - m_new)
    pr = jnp.exp2(s - m_new)
    l_new = alpha * l_old + jnp.sum(pr, axis=-1, keepdims=True)
    pb = pr.astype(BF16)
    rows = k_refs[0].shape[1]
    pv = _dot(pb[:, :rows], v_refs[0][0].astype(BF16))
    for u in range(1, n_pages_step):
        pv += _dot(pb[:, u * rows:(u + 1) * rows], v_refs[u][0].astype(BF16))
    acc_new = alpha * acc_old + pv
    m_ref[...] = m_new
    l_ref[...] = l_new
    acc_ref[...] = acc_new

    kn, vn = kn_ref[0], vn_ref[0]
    prod = q * kn
    sn = jnp.concatenate([jnp.sum(jnp.where(first, prod, 0.0), axis=-1, keepdims=True),
                          jnp.sum(jnp.where(first, 0.0, prod), axis=-1, keepdims=True)], axis=0)
    m_fin = jnp.maximum(m_new, sn)
    a2 = jnp.exp2(m_new - m_fin)
    pn = jnp.exp2(sn - m_fin)
    l_fin = a2 * l_new + pn
    acc_fin = a2 * acc_new + pn * jnp.concatenate([vn, vn], axis=0)
    o2 = acc_fin / l_fin
    o = o2[:n_heads] - lam * o2[n_heads:]
    o_ref[0] = _subln(o, sub_ref[...], lam_init).astype(BF16)


def _rotary_tables(start, length, dk, rows=None):
    angle = 1.0 / (ROPE_BASE ** jnp.linspace(0.0, 1.0, dk // 2, dtype=F32))
    angle = jnp.repeat(angle, 2)
    pos = start + jnp.arange(length, dtype=F32)
    th = pos[:, None] * angle[None, :]
    sign = jnp.where(jnp.arange(dk) % 2 == 0, -1.0, 1.0).astype(F32)
    cos_t = jnp.cos(th)
    sin_t = jnp.sin(th) * sign[None, :]
    if rows is not None:
        cos_t = jnp.broadcast_to(cos_t, (rows, cos_t.shape[1]))
        sin_t = jnp.broadcast_to(sin_t, (rows, sin_t.shape[1]))
    return cos_t, sin_t


def _decay_tables(n_heads, c):
    lg = jnp.log1p(-jnp.exp2(-5.0 - jnp.arange(n_heads, dtype=F32)))
    idx = jnp.arange(c, dtype=F32)
    rel = idx[:, None] - idx[None, :]
    dec_in = jnp.where(rel[None] >= 0, jnp.exp(lg[:, None, None] * jnp.maximum(rel, 0.0)[None]), 0.0)
    dec_q = jnp.exp(lg[:, None] * (idx + 1.0)[None])[:, :, None]
    dec_k = jnp.exp(lg[:, None] * (c - 1.0 - idx)[None])[:, :, None]
    dec_c = jnp.exp(lg * c)
    return dec_c, dec_in, dec_q, dec_k


def kernel(x_prompt, x_sample, state_ret, cache_k, cache_v, page_table, p_prompt, p_sample, norm_mix, ret_w_in, ret_w_out, diff_w_in, diff_w_out, diff_lambda_q1, diff_lambda_k1, diff_lambda_q2, diff_lambda_k2, diff_subln, norm_ffn, w_up, w_down, ple_norm, w_ple_gate, w_ple_proj, final_norm):
    batch, seq, d = x_prompt.shape
    nb, dec_seq, _ = x_sample.shape
    assert dec_seq == 1
    depth = norm_mix.shape[0]
    mp, ms = batch * seq, nb * dec_seq
    yp = x_prompt.reshape(mp, d)
    ys = x_sample.reshape(ms, d)
    row = lambda a: a.reshape(1, -1)
    fn = row(final_norm)

    ret_p, ret_s, kp_rows, vp_rows, ks_rows, vs_rows = [], [], [], [], [], []
    for i in range(depth):
        g_mix = row(norm_mix[i])
        if i % N_MIXERS == 0:
            r = i // N_MIXERS
            n_heads = RET_HEADS
            dk = d // n_heads
            dv = 2 * dk
            w_in = ret_w_in[r].astype(BF16)
            w_out = ret_w_out[r].astype(BF16)
            k_scale = dk ** -0.5
            cos_p, sin_p = _rotary_tables(0, seq, dk)
            proj_p = _ret_proj(yp, g_mix, w_in, cos_p, sin_p, k_scale)
            dec_c, dec_in, dec_q, dec_k = _decay_tables(n_heads, RET_CHUNK)
            og_p, s_p = _ret_scan(proj_p, batch, seq, dec_c, dec_in, dec_q, dec_k)
            yp = _matmul_res(og_p, w_out, yp)
            ret_p.append(s_p.astype(x_prompt.dtype))
            cos_s, sin_s = _rotary_tables(PAST_LEN, 1, dk, rows=ms)
            proj_s = _ret_proj(ys, g_mix, w_in, cos_s, sin_s, k_scale).astype(F32)
            gamma, _, _, _ = _decay_tables(n_heads, 1)
            lane_rep = lambda a: jnp.broadcast_to(a.reshape(ms, n_heads, dk, 1), (ms, n_heads, dk, LANES))
            q_col = lane_rep(proj_s[:, :d])
            k_col = lane_rep(proj_s[:, d:2 * d])
            v_row = proj_s[:, 2 * d:2 * d + n_heads * dv].reshape(ms, n_heads, 1, dv)
            g_row = proj_s[:, 2 * d + n_heads * dv:].reshape(ms, n_heads, 1, dv)
            og_s, s_s = _ret_step(q_col, k_col, v_row, g_row, state_ret[r].astype(F32), gamma)
            ys = _matmul_res(og_s.reshape(ms, n_heads * dv), w_out, ys)
            ret_s.append(s_s.astype(state_ret.dtype))
        else:
            di = i // N_MIXERS
            n_heads = DIFF_HEADS
            dv = d // n_heads
            dh = dv // 2
            lam_init = 0.8 - 0.6 * math.exp(-0.3 * i)
            w_in = diff_w_in[di].astype(BF16)
            w_out = diff_w_out[di].astype(BF16)
            lams = [row(a[di].astype(F32)) for a in (diff_lambda_q1, diff_lambda_k1, diff_lambda_q2, diff_lambda_k2)]
            sub = row(diff_subln[di].astype(F32))
            q_scale = dh ** -0.5 * math.log2(math.e)
            t = ATTN_BLOCK if seq % ATTN_BLOCK == 0 else seq
            q_p, kf_p, vf_p, kb_p, vt_p = _diff_proj(yp, g_mix, w_in, q_scale, t // 2, True)
            q_s, kf_s, vf_s = _diff_proj(ys, g_mix, w_in, q_scale, ms, False)
            n_phys, page_size = cache_k.shape[1], cache_k.shape[2]
            ck = cache_k[di].reshape(n_phys, page_size * n_heads, dv)
            cv = cache_v[di].reshape(n_phys, page_size * n_heads, dv)
            o_p, o_s = _diff_attn(q_p, kb_p, vt_p, batch, seq, t,
                                  q_s.astype(F32).reshape(ms, n_heads, dv), kf_s.reshape(ms, n_heads, dv),
                                  vf_s.reshape(ms, n_heads, dv), ck, cv, page_table, *lams, sub, lam_init)
            yp = _matmul_res(o_p, w_out, yp)
            kp_rows.append(kf_p.reshape(batch, seq, n_heads, dv))
            vp_rows.append(vf_p.reshape(batch, seq, n_heads, dv))
            ys = _matmul_res(o_s.reshape(ms, d), w_out, ys)
            ks_rows.append(kf_s.reshape(nb, dec_seq, n_heads, dv))
            vs_rows.append(vf_s.reshape(nb, dec_seq, n_heads, dv))
        g_ffn = row(norm_ffn[i])
        wu, wd = w_up[i].astype(BF16), w_down[i].astype(BF16)
        yp = _ffn(yp, g_ffn, wu, wd)
        ys = _ffn(ys, g_ffn, wu, wd)
        g_ple = row(ple_norm[i])
        wg, wp = w_ple_gate[i].astype(BF16), w_ple_proj[i].astype(BF16)
        last = i == depth - 1
        yp = _ple(yp, p_prompt[i].reshape(mp, -1), g_ple, wg, wp, fn, last)
        ys = _ple(ys, p_sample[i].reshape(ms, -1), g_ple, wg, wp, fn, last)

    y_prompt = yp.reshape(batch, seq, d)
    y_sample = ys.reshape(nb, dec_seq, d)
    return (y_prompt, y_sample, jnp.stack(ret_p), jnp.stack(ret_s),
            jnp.stack(kp_rows), jnp.stack(vp_rows), jnp.stack(ks_rows), jnp.stack(vs_rows))
```

```python
import functools
import math

import jax
import jax.numpy as jnp
from jax import lax
from jax.experimental import pallas as pl
from jax.experimental.pallas import tpu as pltpu

EPS = 1e-6
NEG_INF = -1e30
N_MIXERS = 2
RET_HEADS = 4
RET_CHUNK = 256
ROPE_BASE = 10000.0
DIFF_HEADS = 8
PAST_LEN = 8192

F32 = jnp.float32
BF16 = jnp.bfloat16
BF16_SUBLANES = 16
VMEM_LIMIT_BYTES = 48 * 1024 * 1024
ROW_TILE = 1024
ATTN_BLOCK = 512
PAGES_PER_STEP = 8
RET_CHUNKS_PER_STEP = 2
RET_PROJ_BLOCK = 1024
RET_STEP_ROWS = 2
FFN_BLOCK = 1024


def _cparams(*sem):
    return pltpu.CompilerParams(dimension_semantics=sem, vmem_limit_bytes=VMEM_LIMIT_BYTES)


def _row_tile(m, pref=None):
    pref = ROW_TILE if pref is None else pref
    return pref if m % pref == 0 else m


def _rms(x, g):
    return x * lax.rsqrt(jnp.mean(x * x, axis=-1, keepdims=True) + EPS) * g


def _dot(a, b):
    return jnp.dot(a, b, preferred_element_type=F32)


def _dot_nt(a, b):
    return lax.dot_general(a, b, (((1,), (1,)), ((), ())), preferred_element_type=F32)


def _dot_tn(a, b):
    return lax.dot_general(a, b, (((0,), (0,)), ((), ())), preferred_element_type=F32)


def _ret_proj_kernel(x_ref, g_ref, w_ref, cos_ref, sin_ref, o_ref, h_ref, *, n_qk_blocks, k_scale):
    j = pl.program_id(1)

    @pl.when(j == 0)
    def _():
        h_ref[...] = _rms(x_ref[...], g_ref[...]).astype(BF16)

    @pl.when(j < n_qk_blocks)
    def _():
        acc = _dot(h_ref[...], w_ref[...])
        dk = cos_ref.shape[1]
        cos_t, sin_t = cos_ref[...], sin_ref[...]
        lane = lax.broadcasted_iota(jnp.int32, cos_t.shape, 1)
        even = (lane & 1) == 0
        scale = jnp.where(j >= n_qk_blocks // 2, k_scale, 1.0).astype(F32)
        for hh in range(acc.shape[1] // dk):
            a = acc[:, hh * dk:(hh + 1) * dk]
            swapped = jnp.where(even, pltpu.roll(a, dk - 1, 1), pltpu.roll(a, 1, 1))
            o_ref[:, hh * dk:(hh + 1) * dk] = ((a * cos_t + swapped * sin_t) * scale).astype(BF16)

    @pl.when(j >= n_qk_blocks)
    def _():
        o_ref[...] = _dot(h_ref[...], w_ref[...]).astype(BF16)


def _ret_proj(x, g, w, cos_t, sin_t, k_scale):
    m, d = x.shape
    n = w.shape[1]
    rows_per_seq, dk = cos_t.shape
    tm = _row_tile(rows_per_seq)
    n_pos_blocks = rows_per_seq // tm
    tn = RET_PROJ_BLOCK if d % RET_PROJ_BLOCK == 0 else dk
    kern = functools.partial(_ret_proj_kernel, n_qk_blocks=2 * d // tn, k_scale=k_scale)
    return pl.pallas_call(
        kern,
        grid=(m // tm, n // tn),
        in_specs=[
            pl.BlockSpec((tm, d), lambda i, j: (i, 0)),
            pl.BlockSpec((1, d), lambda i, j: (0, 0)),
            pl.BlockSpec((d, tn), lambda i, j: (0, j)),
            pl.BlockSpec((tm, dk), lambda i, j: (i % n_pos_blocks, 0)),
            pl.BlockSpec((tm, dk), lambda i, j: (i % n_pos_blocks, 0)),
        ],
        out_specs=pl.BlockSpec((tm, tn), lambda i, j: (i, j)),
        out_shape=jax.ShapeDtypeStruct((m, n), BF16),
        scratch_shapes=[pltpu.VMEM((tm, d), BF16)],
        compiler_params=_cparams("parallel", "arbitrary"),
        name="ret_proj",
    )(x, g, w, cos_t, sin_t)


def _ret_scan_kernel(q_ref, k_ref, v_ref, g_ref, din_ref, dq_ref, dk_ref, o_ref, sfin_ref, s_ref):
    c = pl.program_id(1)

    @pl.when(c == 0)
    def _():
        s_ref[...] = jnp.zeros_like(s_ref)

    chunk = din_ref.shape[1]
    dec_chunk = dq_ref[0, chunk - 1:chunk, :]
    for u in range(q_ref.shape[0] // chunk):
        rows = slice(u * chunk, (u + 1) * chunk)
        q, k, v = q_ref[rows, :], k_ref[rows, :], v_ref[rows, :]
        s_old = s_ref[...]
        scores = _dot_nt(q, k) * din_ref[0]
        o = _dot(scores.astype(BF16), v) + _dot(q, s_old.astype(BF16)) * dq_ref[0]
        k_dec = (k.astype(F32) * dk_ref[0]).astype(BF16)
        s_ref[...] = s_old * dec_chunk + _dot_tn(k_dec, v)
        mu = jnp.mean(o, axis=-1, keepdims=True)
        d = o - mu
        on = d * lax.rsqrt(jnp.mean(d * d, axis=-1, keepdims=True) + EPS)
        gate = g_ref[rows, :].astype(F32)
        o_ref[rows, :] = (gate * jax.nn.sigmoid(gate) * on).astype(BF16)

    @pl.when(c == pl.num_programs(1) - 1)
    def _():
        sfin_ref[0, 0] = s_ref[...]


def _ret_scan(proj, batch, seq, dec_in, dec_q, dec_k):
    n_heads = RET_HEADS
    m = proj.shape[0]
    dk = proj.shape[1] // (6 * n_heads)
    dv = 2 * dk
    assert seq % RET_CHUNK == 0
    chunks_per_step = RET_CHUNKS_PER_STEP if (seq // RET_CHUNK) % RET_CHUNKS_PER_STEP == 0 else 1
    c = RET_CHUNK * chunks_per_step
    nc = seq // c
    kern = _ret_scan_kernel
    row = lambda bh, ci: (bh // n_heads) * nc + ci
    return pl.pallas_call(
        kern,
        grid=(batch * n_heads, nc),
        in_specs=[
            pl.BlockSpec((c, dk), lambda bh, ci: (row(bh, ci), bh % n_heads)),
            pl.BlockSpec((c, dk), lambda bh, ci: (row(bh, ci), n_heads + bh % n_heads)),
            pl.BlockSpec((c, dv), lambda bh, ci: (row(bh, ci), n_heads + bh % n_heads)),
            pl.BlockSpec((c, dv), lambda bh, ci: (row(bh, ci), 2 * n_heads + bh % n_heads)),
            pl.BlockSpec((1, RET_CHUNK, RET_CHUNK), lambda bh, ci: (bh % n_heads, 0, 0)),
            pl.BlockSpec((1, RET_CHUNK, 1), lambda bh, ci: (bh % n_heads, 0, 0)),
            pl.BlockSpec((1, RET_CHUNK, 1), lambda bh, ci: (bh % n_heads, 0, 0)),
        ],
        out_specs=[
            pl.BlockSpec((c, dv), lambda bh, ci: (row(bh, ci), bh % n_heads)),
            pl.BlockSpec((1, 1, dk, dv), lambda bh, ci: (bh // n_heads, bh % n_heads, 0, 0)),
        ],
        out_shape=[
            jax.ShapeDtypeStruct((m, n_heads * dv), BF16),
            jax.ShapeDtypeStruct((batch, n_heads, dk, dv), F32),
        ],
        scratch_shapes=[pltpu.VMEM((dk, dv), F32)],
        compiler_params=_cparams("parallel", "arbitrary"),
        name="ret_scan",
    )(proj, proj, proj, proj, dec_in, dec_q, dec_k)


def _ret_step_kernel(q_ref, k_ref, v_ref, g_ref, s_ref, o_ref, snew_ref, *, gammas):
    for r in range(s_ref.shape[0]):
        for h, gamma in enumerate(gammas):
            q, k = q_ref[r, h], k_ref[r, h]
            v = v_ref[r, h]
            s_old = s_ref[r, h]
            qk = jnp.sum(q * k, axis=0, keepdims=True)
            o = qk * v + gamma * jnp.sum(q * s_old, axis=0, keepdims=True)
            snew_ref[r, h] = s_old * gamma + k * v
            mu = jnp.mean(o, axis=-1, keepdims=True)
            d = o - mu
            on = d * lax.rsqrt(jnp.mean(d * d, axis=-1, keepdims=True) + EPS)
            gate = g_ref[r, h]
            o_ref[r, h] = (gate * jax.nn.sigmoid(gate) * on).astype(BF16)


def _ret_step(q_col, k_col, v_row, g_row, state, gammas):
    nb, n_heads, dk, dv = state.shape
    rb = RET_STEP_ROWS if nb % RET_STEP_ROWS == 0 else 1
    idx = lambda b: (b, 0, 0, 0)
    kern = functools.partial(_ret_step_kernel, gammas=gammas)
    return pl.pallas_call(
        kern,
        grid=(nb // rb,),
        in_specs=[
            pl.BlockSpec((rb, n_heads, dk, 1), idx),
            pl.BlockSpec((rb, n_heads, dk, 1), idx),
            pl.BlockSpec((rb, n_heads, 1, dv), idx),
            pl.BlockSpec((rb, n_heads, 1, dv), idx),
            pl.BlockSpec((rb, n_heads, dk, dv), idx),
        ],
        out_specs=[
            pl.BlockSpec((rb, n_heads, 1, dv), idx),
            pl.BlockSpec((rb, n_heads, dk, dv), idx),
        ],
        out_shape=[
            jax.ShapeDtypeStruct((nb, n_heads, 1, dv), BF16),
            jax.ShapeDtypeStruct(state.shape, F32),
        ],
        compiler_params=_cparams("parallel"),
        name="ret_step",
    )(q_col, k_col, v_row, g_row, state)


def _matmul_res_kernel(a_ref, w_ref, y_ref, o_ref):
    o_ref[...] = y_ref[...] + _dot(a_ref[...], w_ref[...])


def _matmul_res(a, w, y):
    m, k = a.shape
    n = w.shape[1]
    tm = _row_tile(m, 512)
    return pl.pallas_call(
        _matmul_res_kernel,
        grid=(m // tm,),
        in_specs=[
            pl.BlockSpec((tm, k), lambda i: (i, 0)),
            pl.BlockSpec((k, n), lambda i: (0, 0)),
            pl.BlockSpec((tm, n), lambda i: (i, 0)),
        ],
        out_specs=pl.BlockSpec((tm, n), lambda i: (i, 0)),
        out_shape=jax.ShapeDtypeStruct((m, n), F32),
        compiler_params=_cparams("parallel"),
        name="mix_out",
    )(a, w, y)


def _ffn_kernel(y_ref, g_ref, wu_ref, wd_ref, o_ref, h_ref, acc_ref):
    j = pl.program_id(1)

    @pl.when(j == 0)
    def _():
        h_ref[...] = _rms(y_ref[...], g_ref[...]).astype(BF16)
        acc_ref[...] = jnp.zeros_like(acc_ref)

    u = jnp.maximum(_dot(h_ref[...], wu_ref[...]), 0.0)
    acc_ref[...] += _dot((u * u).astype(BF16), wd_ref[...])

    @pl.when(j == pl.num_programs(1) - 1)
    def _():
        o_ref[...] = y_ref[...] + acc_ref[...]


def _ffn(y, g, w_up, w_down):
    m, d = y.shape
    f = w_up.shape[1]
    tm, tf = _row_tile(m), FFN_BLOCK
    return pl.pallas_call(
        _ffn_kernel,
        grid=(m // tm, f // tf),
        in_specs=[
            pl.BlockSpec((tm, d), lambda i, j: (i, 0)),
            pl.BlockSpec((1, d), lambda i, j: (0, 0)),
            pl.BlockSpec((d, tf), lambda i, j: (0, j)),
            pl.BlockSpec((tf, d), lambda i, j: (j, 0)),
        ],
        out_specs=pl.BlockSpec((tm, d), lambda i, j: (i, 0)),
        out_shape=jax.ShapeDtypeStruct((m, d), F32),
        scratch_shapes=[pltpu.VMEM((tm, d), BF16), pltpu.VMEM((tm, d), F32)],
        compiler_params=_cparams("parallel", "arbitrary"),
        name="ffn",
    )(y, g, w_up, w_down)


def _ple_kernel(y_ref, p_ref, g_ref, wg_ref, wp_ref, fn_ref, o_ref, *, final_norm):
    y = y_ref[...]
    gate = jax.nn.sigmoid(_dot(_rms(y, g_ref[...]).astype(BF16), wg_ref[...]))
    out = y + gate * _dot(p_ref[...].astype(BF16), wp_ref[...])
    if final_norm:
        out = _rms(out, fn_ref[...])
    o_ref[...] = out


def _ple(y, p, g, w_gate, w_proj, fn, final_norm):
    m, d = y.shape
    dp = p.shape[1]
    tm = _row_tile(m, 512)
    kern = functools.partial(_ple_kernel, final_norm=final_norm)
    return pl.pallas_call(
        kern,
        grid=(m // tm,),
        in_specs=[
            pl.BlockSpec((tm, d), lambda i: (i, 0)),
            pl.BlockSpec((tm, dp), lambda i: (i, 0)),
            pl.BlockSpec((1, d), lambda i: (0, 0)),
            pl.BlockSpec((d, d), lambda i: (0, 0)),
            pl.BlockSpec((dp, d), lambda i: (0, 0)),
            pl.BlockSpec((1, d), lambda i: (0, 0)),
        ],
        out_specs=pl.BlockSpec((tm, d), lambda i: (i, 0)),
        out_shape=jax.ShapeDtypeStruct((m, d), F32),
        compiler_params=_cparams("parallel"),
        name="ple",
    )(y, p, g, w_gate, w_proj, fn)


def _diff_proj_kernel(x_ref, g_ref, w_ref, q_ref, kf_ref, vf_ref, *rest, q_scale, with_bf16_kv):
    h = _rms(x_ref[...], g_ref[...]).astype(BF16)
    d = h.shape[1]
    q_ref[...] = (_dot(h, w_ref[:, :d]) * q_scale).astype(BF16)
    k = _dot(h, w_ref[:, d:2 * d])
    kf_ref[...] = k
    v = _dot(h, w_ref[:, 2 * d:])
    vf_ref[...] = v
    if with_bf16_kv:
        kb_ref, vt_ref = rest
        kb_ref[...] = k.astype(BF16)
        vt_ref[0] = v.T.astype(BF16)


def _diff_proj(x, g, w, q_scale, tm, with_bf16_kv):
    m, d = x.shape
    kern = functools.partial(_diff_proj_kernel, q_scale=q_scale, with_bf16_kv=with_bf16_kv)
    out_spec = pl.BlockSpec((tm, d), lambda i: (i, 0))
    out_specs = [out_spec] * 3
    out_shape = [
        jax.ShapeDtypeStruct((m, d), BF16),
        jax.ShapeDtypeStruct((m, d), F32),
        jax.ShapeDtypeStruct((m, d), F32),
    ]
    if with_bf16_kv:
        out_specs += [out_spec, pl.BlockSpec((1, d, tm), lambda i: (i, 0, 0))]
        out_shape += [jax.ShapeDtypeStruct((m, d), BF16), jax.ShapeDtypeStruct((m // tm, d, tm), BF16)]
    return pl.pallas_call(
        kern,
        grid=(m // tm,),
        in_specs=[
            pl.BlockSpec((tm, d), lambda i: (i, 0)),
            pl.BlockSpec((1, d), lambda i: (0, 0)),
            pl.BlockSpec((d, 3 * d), lambda i: (0, 0)),
        ],
        out_specs=out_specs,
        out_shape=out_shape,
        compiler_params=_cparams("parallel"),
        name="diff_proj",
    )(x, g, w)


def _lambda_value(lq1, lk1, lq2, lk2, lam_init):
    s1 = jnp.sum(lq1 * lk1, axis=-1, keepdims=True)
    s2 = jnp.sum(lq2 * lk2, axis=-1, keepdims=True)
    return jnp.exp(s1) - jnp.exp(s2) + lam_init


def _subln(o, g, lam_init):
    return _rms(o, g) * (1.0 - lam_init)


def _diff_attn_kernel(pt_ref, q_ref, k_ref, vt_ref, lq1_ref, lk1_ref, lq2_ref, lk2_ref, sub_ref,
                      subrow_ref, qs_ref, kn_ref, vn_ref, *refs, lam_init, dh, n_pages_step, n_groups,
                      n_decode_steps, n_attn_steps, n_q_blocks):
    k_pages = refs[:n_pages_step]
    v_pages = refs[n_pages_step:2 * n_pages_step]
    o_ref, os_ref, m_ref, acc_ref, sa_ref, sb_ref, dm_ref, dl_ref, dacc_ref = refs[2 * n_pages_step:]
    qi = pl.program_id(1)
    step = pl.program_id(0) * n_q_blocks + qi

    @pl.when(step == 0)
    def _():
        dm_ref[...] = jnp.full_like(dm_ref, NEG_INF)
        dl_ref[...] = jnp.zeros_like(dl_ref)
        dacc_ref[...] = jnp.zeros_like(dacc_ref)

    tk = k_ref.shape[1]
    dv = vt_ref.shape[1]
    q = q_ref[...]
    m_ref[...] = jnp.full_like(m_ref, NEG_INF)
    acc_ref[...] = jnp.zeros_like(acc_ref)
    ones_rows = jnp.ones((acc_ref.shape[1] - dv, tk), BF16)

    def scores(ki, s_ref):
        k = k_ref[ki]
        for i in range(2):
            s_ref[i] = _dot_nt(k[:, i * dh:(i + 1) * dh], q[:, i * dh:(i + 1) * dh])

    def update(ki, s_ref, key_offset=None):
        vt = jnp.concatenate([vt_ref[ki], ones_rows], axis=0)
        for i in range(2):
            s = s_ref[i]
            if key_offset is not None:
                key = lax.broadcasted_iota(jnp.int32, s.shape, 0) + key_offset
                qry = lax.broadcasted_iota(jnp.int32, s.shape, 1)
                s = jnp.where(key <= qry, s, NEG_INF)
            m_old = m_ref[i]
            m_new = jnp.maximum(m_old, jnp.max(s, axis=0, keepdims=True))
            alpha = jnp.exp2(m_old - m_new)
            pr = jnp.exp2(s - m_new)
            acc_ref[i] = alpha * acc_ref[i] + _dot(vt, pr.astype(BF16))
            m_ref[i] = m_new

    scores(0, sa_ref)

    def pair(j):
        scores(2 * j + 1, sb_ref)
        update(2 * j, sa_ref)
        scores(2 * j + 2, sa_ref)
        update(2 * j + 1, sb_ref)

    def body(jj, carry):
        pair(2 * jj)
        pair(2 * jj + 1)
        return carry

    lax.fori_loop(0, qi >> 1, body, 0)
    pl.when((qi & 1) == 1)(lambda: pair(qi - 1))
    scores(2 * qi + 1, sb_ref)
    update(2 * qi, sa_ref, 0)
    update(2 * qi + 1, sb_ref, tk)

    lam = _lambda_value(lq1_ref[...], lk1_ref[...], lq2_ref[...], lk2_ref[...], lam_init)
    a1, a2 = acc_ref[0], acc_ref[1]
    o = a1[:dv] / a1[dv:dv + 1] - lam * (a2[:dv] / a2[dv:dv + 1])
    ms = jnp.mean(o * o, axis=0, keepdims=True)
    o = o * lax.rsqrt(ms + EPS) * sub_ref[...] * (1.0 - lam_init)
    o_ref[...] = o.T.astype(BF16)

    def decode():
        _decode_update(step % n_groups == 0, lam, qs_ref, kn_ref, vn_ref, subrow_ref, k_pages, v_pages,
                       os_ref, dm_ref, dl_ref, dacc_ref, lam_init=lam_init, dh=dh)

    if n_decode_steps == n_attn_steps:
        decode()
    else:
        pl.when(step < n_decode_steps)(decode)


def _diff_attn(q, k, vt, batch, seq, t, q_dec, k_new, v_new, cache_k, cache_v, page_table,
               lq1, lk1, lq2, lk2, subln_row, lam_init):
    n_heads = DIFF_HEADS
    assert n_heads & (n_heads - 1) == 0
    m, d = q.shape
    dv = d // n_heads
    dh = dv // 2
    nb = seq // t
    tk = t // 2
    nkb = seq // tk
    assert vt.shape == (m // tk, d, tk)
    n_attn_steps = batch * n_heads * nb
    n_dec, n_pages = page_table.shape
    pps = max(p for p in range(1, PAGES_PER_STEP + 1) if n_pages % p == 0)
    n_groups = n_pages // pps
    n_decode_steps = n_dec * n_groups
    assert n_decode_steps <= n_attn_steps, "decode page groups must fit in the prompt attention grid"
    rows = cache_k.shape[1]
    kern = functools.partial(_diff_attn_kernel, lam_init=lam_init, dh=dh, n_pages_step=pps, n_groups=n_groups,
                             n_decode_steps=n_decode_steps, n_attn_steps=n_attn_steps, n_q_blocks=nb)
    vec = lambda n: pl.BlockSpec((1, n), lambda bh, qi, pt: (0, 0))
    q_map = lambda bh, qi, pt: ((bh // n_heads) * nb + qi, bh % n_heads)
    dstep = lambda bh, qi: jnp.minimum(bh * nb + qi, n_decode_steps - 1)
    tok = pl.BlockSpec((1, n_heads, dv), lambda bh, qi, pt: (dstep(bh, qi) // n_groups, 0, 0))

    def page(u):
        def index(bh, qi, pt):
            s = dstep(bh, qi)
            return (pt[s // n_groups, (s % n_groups) * pps + u], 0, 0)
        return pl.BlockSpec((1, rows, dv), index)

    return pl.pallas_call(
        kern,
        grid_spec=pltpu.PrefetchScalarGridSpec(
            num_scalar_prefetch=1,
            grid=(batch * n_heads, nb),
            in_specs=[
                pl.BlockSpec((t, dv), q_map),
                pl.BlockSpec((nkb, tk, dv), lambda bh, qi, pt: (bh // n_heads, 0, bh % n_heads)),
                pl.BlockSpec((nkb, dv, tk), lambda bh, qi, pt: (bh // n_heads, bh % n_heads, 0)),
                vec(dh), vec(dh), vec(dh), vec(dh),
                pl.BlockSpec((dv, 1), lambda bh, qi, pt: (0, 0)),
                vec(dv), tok, tok, tok,
            ] + [page(u) for u in range(pps)] + [page(u) for u in range(pps)],
            out_specs=[pl.BlockSpec((t, dv), q_map), tok],
            scratch_shapes=[
                pltpu.VMEM((2, 1, t), F32),
                pltpu.VMEM((2, dv + BF16_SUBLANES, t), F32),
                pltpu.VMEM((2, tk, t), F32),
                pltpu.VMEM((2, tk, t), F32),
                pltpu.VMEM((2 * n_heads, 1), F32),
                pltpu.VMEM((2 * n_heads, 1), F32),
                pltpu.VMEM((2 * n_heads, dv), F32),
            ],
        ),
        out_shape=[jax.ShapeDtypeStruct((m, d), BF16), jax.ShapeDtypeStruct((n_dec, n_heads, dv), BF16)],
        compiler_params=_cparams("arbitrary", "arbitrary"),
        name="diff_attn",
    )(page_table, q, k.reshape(m // tk, tk, d), vt, lq1, lk1, lq2, lk2, subln_row.reshape(dv, 1),
      subln_row, q_dec, k_new, v_new, *([cache_k] * pps), *([cache_v] * pps))


def _decode_update(first_group, lam, q_ref, kn_ref, vn_ref, sub_ref, k_refs, v_refs,
                   o_ref, m_ref, l_ref, acc_ref, *, lam_init, dh):
    n_heads = q_ref.shape[1]
    n_pages_step = len(k_refs)
    q = q_ref[0]
    lane = lax.broadcasted_iota(jnp.int32, q.shape, 1)
    first = lane < dh
    qm = jnp.concatenate([jnp.where(first, q, 0.0), jnp.where(first, 0.0, q)], axis=0).astype(BF16)
    s = jnp.concatenate([_dot_nt(qm, kr[0].astype(BF16)) for kr in k_refs], axis=1)
    head_mask = n_heads - 1
    own = ((lax.broadcasted_iota(jnp.int32, s.shape, 1) & head_mask)
           == (lax.broadcasted_iota(jnp.int32, s.shape, 0) & head_mask))
    s = jnp.where(own, s, NEG_INF)
    m_old = jnp.where(first_group, NEG_INF, m_ref[...])
    l_old = jnp.where(first_group, 0.0, l_ref[...])
    acc_old = jnp.where(first_group, 0.0, acc_ref[...])
    m_new = jnp.maximum(m_old, jnp.max(s, axis=-1, keepdims=True))
    alpha = jnp.exp2(m_old - m_new)
    pr = jnp.exp2(s - m_new)
    l_new = alpha * l_old + jnp.sum(pr, axis=-1, keepdims=True)
    pb = pr.astype(BF16)
    rows = k_refs[0].shape[1]
    pv = _dot(pb[:, :rows], v_refs[0][0].astype(BF16))
    for u in range(1, n_pages_step):
        pv += _dot(pb[:, u * rows:(u + 1) * rows], v_refs[u][0].astype(BF16))
    acc_new = alpha * acc_old + pv
    m_ref[...] = m_new
    l_ref[...] = l_new
    acc_ref[...] = acc_new

    kn, vn = kn_ref[0], vn_ref[0]
    prod = q * kn
    sn = jnp.concatenate([jnp.sum(jnp.where(first, prod, 0.0), axis=-1, keepdims=True),
                          jnp.sum(jnp.where(first, 0.0, prod), axis=-1, keepdims=True)], axis=0)
    m_fin = jnp.maximum(m_new, sn)
    a2 = jnp.exp2(m_new - m_fin)
    pn = jnp.exp2(sn - m_fin)
    l_fin = a2 * l_new + pn
    acc_fin = a2 * acc_new + pn * jnp.concatenate([vn, vn], axis=0)
    o2 = acc_fin / l_fin
    o = o2[:n_heads] - lam * o2[n_heads:]
    o_ref[0] = _subln(o, sub_ref[...], lam_init).astype(BF16)


def _rotary_tables(start, length, dk, rows=None):
    angle = 1.0 / (ROPE_BASE ** jnp.linspace(0.0, 1.0, dk // 2, dtype=F32))
    angle = jnp.repeat(angle, 2)
    pos = start + jnp.arange(length, dtype=F32)
    th = pos[:, None] * angle[None, :]
    sign = jnp.where(jnp.arange(dk) % 2 == 0, -1.0, 1.0).astype(F32)
    cos_t = jnp.cos(th)
    sin_t = jnp.sin(th) * sign[None, :]
    if rows is not None:
        cos_t = jnp.broadcast_to(cos_t, (rows, cos_t.shape[1]))
        sin_t = jnp.broadcast_to(sin_t, (rows, sin_t.shape[1]))
    return cos_t, sin_t


def _decay_tables(n_heads, c):
    lg = jnp.log1p(-jnp.exp2(-5.0 - jnp.arange(n_heads, dtype=F32)))
    idx = jnp.arange(c, dtype=F32)
    rel = idx[:, None] - idx[None, :]
    dec_in = jnp.where(rel[None] >= 0, jnp.exp(lg[:, None, None] * jnp.maximum(rel, 0.0)[None]), 0.0)
    dec_q = jnp.exp(lg[:, None] * (idx + 1.0)[None])[:, :, None]
    dec_k = jnp.exp(lg[:, None] * (c - 1.0 - idx)[None])[:, :, None]
    return dec_in, dec_q, dec_k


def kernel(x_prompt, x_sample, state_ret, cache_k, cache_v, page_table, p_prompt, p_sample, norm_mix, ret_w_in, ret_w_out, diff_w_in, diff_w_out, diff_lambda_q1, diff_lambda_k1, diff_lambda_q2, diff_lambda_k2, diff_subln, norm_ffn, w_up, w_down, ple_norm, w_ple_gate, w_ple_proj, final_norm):
    batch, seq, d = x_prompt.shape
    nb, dec_seq, _ = x_sample.shape
    assert dec_seq == 1
    depth = norm_mix.shape[0]
    mp, ms = batch * seq, nb * dec_seq
    yp = x_prompt.reshape(mp, d)
    ys = x_sample.reshape(ms, d)
    row = lambda a: a.reshape(1, -1)
    fn = row(final_norm)

    ret_p, ret_s, kp_rows, vp_rows, ks_rows, vs_rows = [], [], [], [], [], []
    for i in range(depth):
        g_mix = row(norm_mix[i])
        if i % N_MIXERS == 0:
            r = i // N_MIXERS
            n_heads = RET_HEADS
            dk = d // n_heads
            dv = 2 * dk
            w_in = ret_w_in[r].astype(BF16)
            w_out = ret_w_out[r].astype(BF16)
            k_scale = dk ** -0.5
            cos_p, sin_p = _rotary_tables(0, seq, dk)
            proj_p = _ret_proj(yp, g_mix, w_in, cos_p, sin_p, k_scale)
            dec_in, dec_q, dec_k = _decay_tables(n_heads, RET_CHUNK)
            og_p, s_p = _ret_scan(proj_p, batch, seq, dec_in, dec_q, dec_k)
            yp = _matmul_res(og_p, w_out, yp)
            ret_p.append(s_p.astype(x_prompt.dtype))
            cos_s, sin_s = _rotary_tables(PAST_LEN, 1, dk, rows=ms)
            proj_s = _ret_proj(ys, g_mix, w_in, cos_s, sin_s, k_scale).astype(F32)
            gammas = tuple(math.exp(math.log1p(-2.0 ** (-5.0 - h))) for h in range(n_heads))
            q_col = proj_s[:, :d].reshape(ms, n_heads, dk, 1)
            k_col = proj_s[:, d:2 * d].reshape(ms, n_heads, dk, 1)
            v_row = proj_s[:, 2 * d:2 * d + n_heads * dv].reshape(ms, n_heads, 1, dv)
            g_row = proj_s[:, 2 * d + n_heads * dv:].reshape(ms, n_heads, 1, dv)
            og_s, s_s = _ret_step(q_col, k_col, v_row, g_row, state_ret[r].astype(F32), gammas)
            ys = _matmul_res(og_s.reshape(ms, n_heads * dv), w_out, ys)
            ret_s.append(s_s.astype(state_ret.dtype))
        else:
            di = i // N_MIXERS
            n_heads = DIFF_HEADS
            dv = d // n_heads
            dh = dv // 2
            lam_init = 0.8 - 0.6 * math.exp(-0.3 * i)
            w_in = diff_w_in[di].astype(BF16)
            w_out = diff_w_out[di].astype(BF16)
            lams = [row(a[di].astype(F32)) for a in (diff_lambda_q1, diff_lambda_k1, diff_lambda_q2, diff_lambda_k2)]
            sub = row(diff_subln[di].astype(F32))
            q_scale = dh ** -0.5 * math.log2(math.e)
            t = ATTN_BLOCK if seq % ATTN_BLOCK == 0 else seq
            q_p, kf_p, vf_p, kb_p, vt_p = _diff_proj(yp, g_mix, w_in, q_scale, t // 2, True)
            q_s, kf_s, vf_s = _diff_proj(ys, g_mix, w_in, q_scale, ms, False)
            n_phys, page_size = cache_k.shape[1], cache_k.shape[2]
            ck = cache_k[di].reshape(n_phys, page_size * n_heads, dv)
            cv = cache_v[di].reshape(n_phys, page_size * n_heads, dv)
            o_p, o_s = _diff_attn(q_p, kb_p, vt_p, batch, seq, t,
                                  q_s.astype(F32).reshape(ms, n_heads, dv), kf_s.reshape(ms, n_heads, dv),
                                  vf_s.reshape(ms, n_heads, dv), ck, cv, page_table, *lams, sub, lam_init)
            yp = _matmul_res(o_p, w_out, yp)
            kp_rows.append(kf_p.reshape(batch, seq, n_heads, dv))
            vp_rows.append(vf_p.reshape(batch, seq, n_heads, dv))
            ys = _matmul_res(o_s.reshape(ms, d), w_out, ys)
            ks_rows.append(kf_s.reshape(nb, dec_seq, n_heads, dv))
            vs_rows.append(vf_s.reshape(nb, dec_seq, n_heads, dv))
        g_ffn = row(norm_ffn[i])
        wu, wd = w_up[i].astype(BF16), w_down[i].astype(BF16)
        yp = _ffn(yp, g_ffn, wu, wd)
        ys = _ffn(ys, g_ffn, wu, wd)
        g_ple = row(ple_norm[i])
        wg, wp = w_ple_gate[i].astype(BF16), w_ple_proj[i].astype(BF16)
        last = i == depth - 1
        yp = _ple(yp, p_prompt[i].reshape(mp, -1), g_ple, wg, wp, fn, last)
        ys = _ple(ys, p_sample[i].reshape(ms, -1), g_ple, wg, wp, fn, last)

    y_prompt = yp.reshape(batch, seq, d)
    y_sample = ys.reshape(nb, dec_seq, d)
    return (y_prompt, y_sample, jnp.stack(ret_p), jnp.stack(ret_s),
            jnp.stack(kp_rows), jnp.stack(vp_rows), jnp.stack(ks_rows), jnp.stack(vs_rows))
```

```python
import functools
import math

import jax
import jax.numpy as jnp
from jax import lax
from jax.experimental import pallas as pl
from jax.experimental.pallas import tpu as pltpu

EPS = 1e-6
NEG_INF = -1e30
N_MIXERS = 2
RET_HEADS = 4
RET_CHUNK = 256
ROPE_BASE = 10000.0
DIFF_HEADS = 8
PAST_LEN = 8192

F32 = jnp.float32
BF16 = jnp.bfloat16
BF16_SUBLANES = 16
VMEM_LIMIT_BYTES = 48 * 1024 * 1024
ROW_TILE = 1024
ATTN_BLOCK = 512
PAGES_PER_STEP = 8
RET_CHUNKS_PER_STEP = 4
RET_PROJ_BLOCK = 1024
RET_STEP_ROWS = 2
COLUMN_TILE = 128
FFN_BLOCK = 1024


def _cparams(*sem):
    return pltpu.CompilerParams(dimension_semantics=sem, vmem_limit_bytes=VMEM_LIMIT_BYTES)


def _row_tile(m, pref=None):
    pref = ROW_TILE if pref is None else pref
    return pref if m % pref == 0 else m


def _rms(x, g):
    return x * lax.rsqrt(jnp.mean(x * x, axis=-1, keepdims=True) + EPS) * g


def _dot(a, b):
    return jnp.dot(a, b, preferred_element_type=F32)


def _dot_nt(a, b):
    return lax.dot_general(a, b, (((1,), (1,)), ((), ())), preferred_element_type=F32)


def _dot_tn(a, b):
    return lax.dot_general(a, b, (((0,), (0,)), ((), ())), preferred_element_type=F32)


def _ret_proj_kernel(x_ref, g_ref, w_ref, cos_ref, sin_ref, o_ref, h_ref, *, n_qk_blocks, k_scale, dk, split_halves):
    j = pl.program_id(1)

    @pl.when(j == 0)
    def _():
        h_ref[...] = _rms(x_ref[...], g_ref[...]).astype(BF16)

    @pl.when(j < n_qk_blocks)
    def _():
        acc = _dot(h_ref[...], w_ref[...])
        cos_t, sin_t = cos_ref[...], sin_ref[...]
        scale = jnp.where(j >= n_qk_blocks // 2, k_scale, 1.0).astype(F32)
        for hh in range(acc.shape[1] // dk):
            a = acc[:, hh * dk:(hh + 1) * dk]
            if split_halves:
                half = dk // 2
                ev, od = a[:, :half], a[:, half:]
                o_ref[:, hh * dk:hh * dk + half] = ((ev * cos_t - od * sin_t) * scale).astype(BF16)
                o_ref[:, hh * dk + half:(hh + 1) * dk] = ((od * cos_t + ev * sin_t) * scale).astype(BF16)
            else:
                even = (lax.broadcasted_iota(jnp.int32, a.shape, 1) & 1) == 0
                swapped = jnp.where(even, pltpu.roll(a, dk - 1, 1), pltpu.roll(a, 1, 1))
                o_ref[:, hh * dk:(hh + 1) * dk] = ((a * cos_t + swapped * sin_t) * scale).astype(BF16)

    @pl.when(j >= n_qk_blocks)
    def _():
        o_ref[...] = _dot(h_ref[...], w_ref[...]).astype(BF16)


def _ret_proj(x, g, w, cos_t, sin_t, k_scale, dk, split_halves):
    m, d = x.shape
    n = w.shape[1]
    rows_per_seq, tw = cos_t.shape
    assert tw == (dk // 2 if split_halves else dk)
    tm = _row_tile(rows_per_seq)
    n_pos_blocks = rows_per_seq // tm
    tn = RET_PROJ_BLOCK if d % RET_PROJ_BLOCK == 0 else dk
    kern = functools.partial(_ret_proj_kernel, n_qk_blocks=2 * d // tn, k_scale=k_scale, dk=dk,
                             split_halves=split_halves)
    return pl.pallas_call(
        kern,
        grid=(m // tm, n // tn),
        in_specs=[
            pl.BlockSpec((tm, d), lambda i, j: (i, 0)),
            pl.BlockSpec((1, d), lambda i, j: (0, 0)),
            pl.BlockSpec((d, tn), lambda i, j: (0, j)),
            pl.BlockSpec((tm, tw), lambda i, j: (i % n_pos_blocks, 0)),
            pl.BlockSpec((tm, tw), lambda i, j: (i % n_pos_blocks, 0)),
        ],
        out_specs=pl.BlockSpec((tm, tn), lambda i, j: (i, j)),
        out_shape=jax.ShapeDtypeStruct((m, n), BF16),
        scratch_shapes=[pltpu.VMEM((tm, d), BF16)],
        compiler_params=_cparams("parallel", "arbitrary"),
        name="ret_proj",
    )(x, g, w, cos_t, sin_t)


def _ret_scan_kernel(q_ref, k_ref, v_ref, g_ref, din_ref, dq_ref, dk_ref, o_ref, sfin_ref, s_ref):
    c = pl.program_id(1)

    @pl.when(c == 0)
    def _():
        s_ref[...] = jnp.zeros_like(s_ref)

    chunk = din_ref.shape[1]
    dec_chunk = dq_ref[0, chunk - 1:chunk, :]
    for u in range(q_ref.shape[0] // chunk):
        rows = slice(u * chunk, (u + 1) * chunk)
        q, k, v = q_ref[rows, :], k_ref[rows, :], v_ref[rows, :]
        s_old = s_ref[...]
        scores = _dot_nt(q, k) * din_ref[0]
        o = _dot(scores.astype(BF16), v) + _dot(q, s_old.astype(BF16)) * dq_ref[0]
        k_dec = (k.astype(F32) * dk_ref[0]).astype(BF16)
        s_ref[...] = s_old * dec_chunk + _dot_tn(k_dec, v)
        mu = jnp.mean(o, axis=-1, keepdims=True)
        d = o - mu
        on = d * lax.rsqrt(jnp.mean(d * d, axis=-1, keepdims=True) + EPS)
        gate = g_ref[rows, :].astype(F32)
        o_ref[rows, :] = (gate * jax.nn.sigmoid(gate) * on).astype(BF16)

    @pl.when(c == pl.num_programs(1) - 1)
    def _():
        sfin_ref[0, 0] = s_ref[...]


def _ret_scan(proj, batch, seq, dec_in, dec_q, dec_k):
    n_heads = RET_HEADS
    m = proj.shape[0]
    dk = proj.shape[1] // (6 * n_heads)
    dv = 2 * dk
    assert seq % RET_CHUNK == 0
    chunks_per_step = RET_CHUNKS_PER_STEP if (seq // RET_CHUNK) % RET_CHUNKS_PER_STEP == 0 else 1
    c = RET_CHUNK * chunks_per_step
    nc = seq // c
    kern = _ret_scan_kernel
    row = lambda bh, ci: (bh // n_heads) * nc + ci
    return pl.pallas_call(
        kern,
        grid=(batch * n_heads, nc),
        in_specs=[
            pl.BlockSpec((c, dk), lambda bh, ci: (row(bh, ci), bh % n_heads)),
            pl.BlockSpec((c, dk), lambda bh, ci: (row(bh, ci), n_heads + bh % n_heads)),
            pl.BlockSpec((c, dv), lambda bh, ci: (row(bh, ci), n_heads + bh % n_heads)),
            pl.BlockSpec((c, dv), lambda bh, ci: (row(bh, ci), 2 * n_heads + bh % n_heads)),
            pl.BlockSpec((1, RET_CHUNK, RET_CHUNK), lambda bh, ci: (bh % n_heads, 0, 0)),
            pl.BlockSpec((1, RET_CHUNK, 1), lambda bh, ci: (bh % n_heads, 0, 0)),
            pl.BlockSpec((1, RET_CHUNK, 1), lambda bh, ci: (bh % n_heads, 0, 0)),
        ],
        out_specs=[
            pl.BlockSpec((c, dv), lambda bh, ci: (row(bh, ci), bh % n_heads)),
            pl.BlockSpec((1, 1, dk, dv), lambda bh, ci: (bh // n_heads, bh % n_heads, 0, 0)),
        ],
        out_shape=[
            jax.ShapeDtypeStruct((m, n_heads * dv), BF16),
            jax.ShapeDtypeStruct((batch, n_heads, dk, dv), F32),
        ],
        scratch_shapes=[pltpu.VMEM((dk, dv), F32)],
        compiler_params=_cparams("parallel", "arbitrary"),
        name="ret_scan",
    )(proj, proj, proj, proj, dec_in, dec_q, dec_k)


def _ret_step_kernel(q_ref, k_ref, v_ref, g_ref, s_ref, o_ref, snew_ref, *, gammas):
    dk, dv = s_ref.shape[2], s_ref.shape[3]

    def column(row):
        tile = jnp.broadcast_to(row, (COLUMN_TILE, dk)).T
        return jnp.concatenate([tile] * (dv // COLUMN_TILE), axis=1)

    for r in range(s_ref.shape[0]):
        for h, gamma in enumerate(gammas):
            q_row, k_row = q_ref[r, h], k_ref[r, h]
            q, k = column(q_row), column(k_row)
            v = v_ref[r, h]
            s_old = s_ref[r, h]
            qk = jnp.sum(q_row * k_row, axis=-1, keepdims=True)
            o = qk * v + gamma * jnp.sum(q * s_old, axis=0, keepdims=True)
            snew_ref[r, h] = s_old * gamma + k * v
            mu = jnp.mean(o, axis=-1, keepdims=True)
            d = o - mu
            on = d * lax.rsqrt(jnp.mean(d * d, axis=-1, keepdims=True) + EPS)
            gate = g_ref[r, h]
            o_ref[r, h] = (gate * jax.nn.sigmoid(gate) * on).astype(BF16)


def _ret_step(q_col, k_col, v_row, g_row, state, gammas):
    nb, n_heads, dk, dv = state.shape
    rb = RET_STEP_ROWS if nb % RET_STEP_ROWS == 0 else 1
    idx = lambda b: (b, 0, 0, 0)
    kern = functools.partial(_ret_step_kernel, gammas=gammas)
    return pl.pallas_call(
        kern,
        grid=(nb // rb,),
        in_specs=[
            pl.BlockSpec((rb, n_heads, 1, dk), idx),
            pl.BlockSpec((rb, n_heads, 1, dk), idx),
            pl.BlockSpec((rb, n_heads, 1, dv), idx),
            pl.BlockSpec((rb, n_heads, 1, dv), idx),
            pl.BlockSpec((rb, n_heads, dk, dv), idx),
        ],
        out_specs=[
            pl.BlockSpec((rb, n_heads, 1, dv), idx),
            pl.BlockSpec((rb, n_heads, dk, dv), idx),
        ],
        out_shape=[
            jax.ShapeDtypeStruct((nb, n_heads, 1, dv), BF16),
            jax.ShapeDtypeStruct(state.shape, F32),
        ],
        compiler_params=_cparams("parallel"),
        name="ret_step",
    )(q_col, k_col, v_row, g_row, state)


def _matmul_res_kernel(a_ref, w_ref, y_ref, o_ref):
    o_ref[...] = y_ref[...] + _dot(a_ref[...], w_ref[...])


def _matmul_res(a, w, y):
    m, k = a.shape
    n = w.shape[1]
    tm = _row_tile(m, 512)
    return pl.pallas_call(
        _matmul_res_kernel,
        grid=(m // tm,),
        in_specs=[
            pl.BlockSpec((tm, k), lambda i: (i, 0)),
            pl.BlockSpec((k, n), lambda i: (0, 0)),
            pl.BlockSpec((tm, n), lambda i: (i, 0)),
        ],
        out_specs=pl.BlockSpec((tm, n), lambda i: (i, 0)),
        out_shape=jax.ShapeDtypeStruct((m, n), F32),
        compiler_params=_cparams("parallel"),
        name="mix_out",
    )(a, w, y)


def _ffn_kernel(y_ref, g_ref, wu_ref, wd_ref, o_ref, h_ref, acc_ref):
    j = pl.program_id(1)

    @pl.when(j == 0)
    def _():
        h_ref[...] = _rms(y_ref[...], g_ref[...]).astype(BF16)
        acc_ref[...] = jnp.zeros_like(acc_ref)

    u = jnp.maximum(_dot(h_ref[...], wu_ref[...]), 0.0)
    acc_ref[...] += _dot((u * u).astype(BF16), wd_ref[...])

    @pl.when(j == pl.num_programs(1) - 1)
    def _():
        o_ref[...] = y_ref[...] + acc_ref[...]


def _ffn(y, g, w_up, w_down):
    m, d = y.shape
    f = w_up.shape[1]
    tm, tf = _row_tile(m), FFN_BLOCK
    return pl.pallas_call(
        _ffn_kernel,
        grid=(m // tm, f // tf),
        in_specs=[
            pl.BlockSpec((tm, d), lambda i, j: (i, 0)),
            pl.BlockSpec((1, d), lambda i, j: (0, 0)),
            pl.BlockSpec((d, tf), lambda i, j: (0, j)),
            pl.BlockSpec((tf, d), lambda i, j: (j, 0)),
        ],
        out_specs=pl.BlockSpec((tm, d), lambda i, j: (i, 0)),
        out_shape=jax.ShapeDtypeStruct((m, d), F32),
        scratch_shapes=[pltpu.VMEM((tm, d), BF16), pltpu.VMEM((tm, d), F32)],
        compiler_params=_cparams("parallel", "arbitrary"),
        name="ffn",
    )(y, g, w_up, w_down)


def _ple_kernel(y_ref, p_ref, g_ref, wg_ref, wp_ref, fn_ref, o_ref, *, final_norm):
    y = y_ref[...]
    gate = jax.nn.sigmoid(_dot(_rms(y, g_ref[...]).astype(BF16), wg_ref[...]))
    out = y + gate * _dot(p_ref[0].astype(BF16), wp_ref[...])
    if final_norm:
        out = _rms(out, fn_ref[...])
    o_ref[...] = out


def _ple(y, p_layers, layer, g, w_gate, w_proj, fn, final_norm):
    m, d = y.shape
    dp = p_layers.shape[2]
    tm = _row_tile(m, 512)
    kern = functools.partial(_ple_kernel, final_norm=final_norm)
    return pl.pallas_call(
        kern,
        grid=(m // tm,),
        in_specs=[
            pl.BlockSpec((tm, d), lambda i: (i, 0)),
            pl.BlockSpec((1, tm, dp), lambda i: (layer, i, 0)),
            pl.BlockSpec((1, d), lambda i: (0, 0)),
            pl.BlockSpec((d, d), lambda i: (0, 0)),
            pl.BlockSpec((dp, d), lambda i: (0, 0)),
            pl.BlockSpec((1, d), lambda i: (0, 0)),
        ],
        out_specs=pl.BlockSpec((tm, d), lambda i: (i, 0)),
        out_shape=jax.ShapeDtypeStruct((m, d), F32),
        compiler_params=_cparams("parallel"),
        name="ple",
    )(y, p_layers, g, w_gate, w_proj, fn)


def _diff_proj_kernel(x_ref, g_ref, w_ref, q_ref, kf_ref, vf_ref, *rest, q_scale, with_bf16_kv):
    h = _rms(x_ref[...], g_ref[...]).astype(BF16)
    d = h.shape[1]
    q_ref[...] = (_dot(h, w_ref[:, :d]) * q_scale).astype(BF16)
    k = _dot(h, w_ref[:, d:2 * d])
    kf_ref[...] = k
    v = _dot(h, w_ref[:, 2 * d:])
    vf_ref[...] = v
    if with_bf16_kv:
        kb_ref, vt_ref = rest
        kb_ref[...] = k.astype(BF16)
        vt_ref[0] = v.T.astype(BF16)


def _diff_proj(x, g, w, q_scale, tm, with_bf16_kv):
    m, d = x.shape
    kern = functools.partial(_diff_proj_kernel, q_scale=q_scale, with_bf16_kv=with_bf16_kv)
    out_spec = pl.BlockSpec((tm, d), lambda i: (i, 0))
    out_specs = [out_spec] * 3
    out_shape = [
        jax.ShapeDtypeStruct((m, d), BF16),
        jax.ShapeDtypeStruct((m, d), F32),
        jax.ShapeDtypeStruct((m, d), F32),
    ]
    if with_bf16_kv:
        out_specs += [out_spec, pl.BlockSpec((1, d, tm), lambda i: (i, 0, 0))]
        out_shape += [jax.ShapeDtypeStruct((m, d), BF16), jax.ShapeDtypeStruct((m // tm, d, tm), BF16)]
    return pl.pallas_call(
        kern,
        grid=(m // tm,),
        in_specs=[
            pl.BlockSpec((tm, d), lambda i: (i, 0)),
            pl.BlockSpec((1, d), lambda i: (0, 0)),
            pl.BlockSpec((d, 3 * d), lambda i: (0, 0)),
        ],
        out_specs=out_specs,
        out_shape=out_shape,
        compiler_params=_cparams("parallel"),
        name="diff_proj",
    )(x, g, w)


def _lambda_value(lq1, lk1, lq2, lk2, lam_init):
    s1 = jnp.sum(lq1 * lk1, axis=-1, keepdims=True)
    s2 = jnp.sum(lq2 * lk2, axis=-1, keepdims=True)
    return jnp.exp(s1) - jnp.exp(s2) + lam_init


def _subln(o, g, lam_init):
    return _rms(o, g) * (1.0 - lam_init)


def _diff_attn_kernel(pt_ref, q_ref, k_ref, vt_ref, lq1_ref, lk1_ref, lq2_ref, lk2_ref, sub_ref,
                      subrow_ref, qs_ref, kn_ref, vn_ref, *refs, lam_init, dh, n_pages_step, n_groups,
                      n_decode_steps, n_attn_steps, n_q_blocks):
    k_pages = refs[:n_pages_step]
    v_pages = refs[n_pages_step:2 * n_pages_step]
    (o_ref, os_ref, m_ref, acc_ref, sa_ref, sb_ref, mxa_ref, mxb_ref,
     dm_ref, dl_ref, dacc_ref) = refs[2 * n_pages_step:]
    qi = pl.program_id(1)
    step = pl.program_id(0) * n_q_blocks + qi

    @pl.when(step == 0)
    def _():
        dm_ref[...] = jnp.full_like(dm_ref, NEG_INF)
        dl_ref[...] = jnp.zeros_like(dl_ref)
        dacc_ref[...] = jnp.zeros_like(dacc_ref)

    tk = k_ref.shape[1]
    dv = vt_ref.shape[1]
    q = q_ref[...]
    m_ref[...] = jnp.full_like(m_ref, NEG_INF)
    acc_ref[...] = jnp.zeros_like(acc_ref)
    ones_rows = jnp.ones((acc_ref.shape[1] - dv, tk), BF16)

    def scores(ki, s_ref, mx_ref):
        k = k_ref[ki]
        for i in range(2):
            s = _dot_nt(k[:, i * dh:(i + 1) * dh], q[:, i * dh:(i + 1) * dh])
            s_ref[i] = s
            mx_ref[i] = jnp.max(s, axis=0, keepdims=True)

    def update(ki, s_ref, mx_ref, key_offset=None):
        vt = jnp.concatenate([vt_ref[ki], ones_rows], axis=0)
        for i in range(2):
            s = s_ref[i]
            if key_offset is None:
                mx = mx_ref[i]
            else:
                key = lax.broadcasted_iota(jnp.int32, s.shape, 0) + key_offset
                qry = lax.broadcasted_iota(jnp.int32, s.shape, 1)
                s = jnp.where(key <= qry, s, NEG_INF)
                mx = jnp.max(s, axis=0, keepdims=True)
            m_old = m_ref[i]
            m_new = jnp.maximum(m_old, mx)
            alpha = jnp.exp2(m_old - m_new)
            pr = jnp.exp2(s - m_new)
            acc_ref[i] = alpha * acc_ref[i] + _dot(vt, pr.astype(BF16))
            m_ref[i] = m_new

    scores(0, sa_ref, mxa_ref)

    def pair(j):
        scores(2 * j + 1, sb_ref, mxb_ref)
        update(2 * j, sa_ref, mxa_ref)
        scores(2 * j + 2, sa_ref, mxa_ref)
        update(2 * j + 1, sb_ref, mxb_ref)

    def body(jj, carry):
        pair(2 * jj)
        pair(2 * jj + 1)
        return carry

    lax.fori_loop(0, qi >> 1, body, 0)
    pl.when((qi & 1) == 1)(lambda: pair(qi - 1))
    scores(2 * qi + 1, sb_ref, mxb_ref)
    update(2 * qi, sa_ref, mxa_ref, 0)
    update(2 * qi + 1, sb_ref, mxb_ref, tk)

    lam = _lambda_value(lq1_ref[...], lk1_ref[...], lq2_ref[...], lk2_ref[...], lam_init)
    a1, a2 = acc_ref[0], acc_ref[1]
    o = a1[:dv] / a1[dv:dv + 1] - lam * (a2[:dv] / a2[dv:dv + 1])
    ms = jnp.mean(o * o, axis=0, keepdims=True)
    o = o * lax.rsqrt(ms + EPS) * sub_ref[...] * (1.0 - lam_init)
    o_ref[...] = o.T.astype(BF16)

    def decode():
        _decode_update(step % n_groups == 0, lam, qs_ref, kn_ref, vn_ref, subrow_ref, k_pages, v_pages,
                       os_ref, dm_ref, dl_ref, dacc_ref, lam_init=lam_init, dh=dh)

    if n_decode_steps == n_attn_steps:
        decode()
    else:
        pl.when(step < n_decode_steps)(decode)


def _diff_attn(q, k, vt, batch, seq, t, q_dec, k_new, v_new, cache_k, cache_v, page_table,
               lq1, lk1, lq2, lk2, subln_row, lam_init):
    n_heads = DIFF_HEADS
    assert n_heads & (n_heads - 1) == 0
    m, d = q.shape
    dv = d // n_heads
    dh = dv // 2
    nb = seq // t
    tk = t // 2
    nkb = seq // tk
    assert vt.shape == (m // tk, d, tk)
    n_attn_steps = batch * n_heads * nb
    n_dec, n_pages = page_table.shape
    pps = max(p for p in range(1, PAGES_PER_STEP + 1) if n_pages % p == 0)
    n_groups = n_pages // pps
    n_decode_steps = n_dec * n_groups
    assert n_decode_steps <= n_attn_steps, "decode page groups must fit in the prompt attention grid"
    rows = cache_k.shape[1]
    kern = functools.partial(_diff_attn_kernel, lam_init=lam_init, dh=dh, n_pages_step=pps, n_groups=n_groups,
                             n_decode_steps=n_decode_steps, n_attn_steps=n_attn_steps, n_q_blocks=nb)
    vec = lambda n: pl.BlockSpec((1, n), lambda bh, qi, pt: (0, 0))
    q_map = lambda bh, qi, pt: ((bh // n_heads) * nb + qi, bh % n_heads)
    dstep = lambda bh, qi: jnp.minimum(bh * nb + qi, n_decode_steps - 1)
    tok = pl.BlockSpec((1, n_heads, dv), lambda bh, qi, pt: (dstep(bh, qi) // n_groups, 0, 0))

    def page(u):
        def index(bh, qi, pt):
            s = dstep(bh, qi)
            return (pt[s // n_groups, (s % n_groups) * pps + u], 0, 0)
        return pl.BlockSpec((1, rows, dv), index)

    return pl.pallas_call(
        kern,
        grid_spec=pltpu.PrefetchScalarGridSpec(
            num_scalar_prefetch=1,
            grid=(batch * n_heads, nb),
            in_specs=[
                pl.BlockSpec((t, dv), q_map),
                pl.BlockSpec((nkb, tk, dv), lambda bh, qi, pt: (bh // n_heads, 0, bh % n_heads)),
                pl.BlockSpec((nkb, dv, tk), lambda bh, qi, pt: (bh // n_heads, bh % n_heads, 0)),
                vec(dh), vec(dh), vec(dh), vec(dh),
                pl.BlockSpec((dv, 1), lambda bh, qi, pt: (0, 0)),
                vec(dv), tok, tok, tok,
            ] + [page(u) for u in range(pps)] + [page(u) for u in range(pps)],
            out_specs=[pl.BlockSpec((t, dv), q_map), tok],
            scratch_shapes=[
                pltpu.VMEM((2, 1, t), F32),
                pltpu.VMEM((2, dv + BF16_SUBLANES, t), F32),
                pltpu.VMEM((2, tk, t), F32),
                pltpu.VMEM((2, tk, t), F32),
                pltpu.VMEM((2, 1, t), F32),
                pltpu.VMEM((2, 1, t), F32),
                pltpu.VMEM((2 * n_heads, 1), F32),
                pltpu.VMEM((2 * n_heads, 1), F32),
                pltpu.VMEM((2 * n_heads, dv), F32),
            ],
        ),
        out_shape=[jax.ShapeDtypeStruct((m, d), BF16), jax.ShapeDtypeStruct((n_dec, n_heads, dv), BF16)],
        compiler_params=_cparams("arbitrary", "arbitrary"),
        name="diff_attn",
    )(page_table, q, k.reshape(m // tk, tk, d), vt, lq1, lk1, lq2, lk2, subln_row.reshape(dv, 1),
      subln_row, q_dec, k_new, v_new, *([cache_k] * pps), *([cache_v] * pps))


def _decode_update(first_group, lam, q_ref, kn_ref, vn_ref, sub_ref, k_refs, v_refs,
                   o_ref, m_ref, l_ref, acc_ref, *, lam_init, dh):
    n_heads = q_ref.shape[1]
    n_pages_step = len(k_refs)
    q = q_ref[0]
    lane = lax.broadcasted_iota(jnp.int32, q.shape, 1)
    first = lane < dh
    qm = jnp.concatenate([jnp.where(first, q, 0.0), jnp.where(first, 0.0, q)], axis=0).astype(BF16)
    s = jnp.concatenate([_dot_nt(qm, kr[0].astype(BF16)) for kr in k_refs], axis=1)
    head_mask = n_heads - 1
    own = ((lax.broadcasted_iota(jnp.int32, s.shape, 1) & head_mask)
           == (lax.broadcasted_iota(jnp.int32, s.shape, 0) & head_mask))
    s = jnp.where(own, s, NEG_INF)
    m_old = jnp.where(first_group, NEG_INF, m_ref[...])
    l_old = jnp.where(first_group, 0.0, l_ref[...])
    acc_old = jnp.where(first_group, 0.0, acc_ref[...])
    m_new = jnp.maximum(m_old, jnp.max(s, axis=-1, keepdims=True))
    alpha = jnp.exp2(m_old - m_new)
    pr = jnp.exp2(s - m_new)
    l_new = alpha * l_old + jnp.sum(pr, axis=-1, keepdims=True)
    pb = pr.astype(BF16)
    rows = k_refs[0].shape[1]
    pv = _dot(pb[:, :rows], v_refs[0][0].astype(BF16))
    for u in range(1, n_pages_step):
        pv += _dot(pb[:, u * rows:(u + 1) * rows], v_refs[u][0].astype(BF16))
    acc_new = alpha * acc_old + pv
    m_ref[...] = m_new
    l_ref[...] = l_new
    acc_ref[...] = acc_new

    kn, vn = kn_ref[0], vn_ref[0]
    prod = q * kn
    sn = jnp.concatenate([jnp.sum(jnp.where(first, prod, 0.0), axis=-1, keepdims=True),
                          jnp.sum(jnp.where(first, 0.0, prod), axis=-1, keepdims=True)], axis=0)
    m_fin = jnp.maximum(m_new, sn)
    a2 = jnp.exp2(m_new - m_fin)
    pn = jnp.exp2(sn - m_fin)
    l_fin = a2 * l_new + pn
    acc_fin = a2 * acc_new + pn * jnp.concatenate([vn, vn], axis=0)
    o2 = acc_fin / l_fin
    o = o2[:n_heads] - lam * o2[n_heads:]
    o_ref[0] = _subln(o, sub_ref[...], lam_init).astype(BF16)


def _rotary_tables(start, length, dk, rows=None):
    angle = 1.0 / (ROPE_BASE ** jnp.linspace(0.0, 1.0, dk // 2, dtype=F32))
    angle = jnp.repeat(angle, 2)
    pos = start + jnp.arange(length, dtype=F32)
    th = pos[:, None] * angle[None, :]
    sign = jnp.where(jnp.arange(dk) % 2 == 0, -1.0, 1.0).astype(F32)
    cos_t = jnp.cos(th)
    sin_t = jnp.sin(th) * sign[None, :]
    if rows is not None:
        cos_t = jnp.broadcast_to(cos_t, (rows, cos_t.shape[1]))
        sin_t = jnp.broadcast_to(sin_t, (rows, sin_t.shape[1]))
    return cos_t, sin_t


def _rotary_half_tables(length, dk, block=128):
    angle = 1.0 / (ROPE_BASE ** jnp.linspace(0.0, 1.0, dk // 2, dtype=F32))
    n_hi = -(-length // block)
    th_hi = (jnp.arange(n_hi, dtype=F32) * block)[:, None] * angle[None, :]
    th_lo = jnp.arange(block, dtype=F32)[:, None] * angle[None, :]
    ch, sh = jnp.cos(th_hi)[:, None, :], jnp.sin(th_hi)[:, None, :]
    cl, sl = jnp.cos(th_lo)[None, :, :], jnp.sin(th_lo)[None, :, :]
    cos_t = (ch * cl - sh * sl).reshape(n_hi * block, dk // 2)[:length]
    sin_t = (sh * cl + ch * sl).reshape(n_hi * block, dk // 2)[:length]
    return cos_t, sin_t


def _split_even_odd_columns(w, dk):
    rows, cols = w.shape
    return w.reshape(rows, cols // dk, dk // 2, 2).swapaxes(2, 3).reshape(rows, cols)


def _merge_even_odd_rows(s, dk):
    lead = s.shape[:-2]
    return s.reshape(*lead, 2, dk // 2, s.shape[-1]).swapaxes(-3, -2).reshape(*lead, dk, s.shape[-1])


def _decay_tables(n_heads, c):
    lg = jnp.log1p(-jnp.exp2(-5.0 - jnp.arange(n_heads, dtype=F32)))
    idx = jnp.arange(c, dtype=F32)
    rel = idx[:, None] - idx[None, :]
    dec_in = jnp.where(rel[None] >= 0, jnp.exp(lg[:, None, None] * jnp.maximum(rel, 0.0)[None]), 0.0)
    dec_q = jnp.exp(lg[:, None] * (idx + 1.0)[None])[:, :, None]
    dec_k = jnp.exp(lg[:, None] * (c - 1.0 - idx)[None])[:, :, None]
    return dec_in, dec_q, dec_k


def kernel(x_prompt, x_sample, state_ret, cache_k, cache_v, page_table, p_prompt, p_sample, norm_mix, ret_w_in, ret_w_out, diff_w_in, diff_w_out, diff_lambda_q1, diff_lambda_k1, diff_lambda_q2, diff_lambda_k2, diff_subln, norm_ffn, w_up, w_down, ple_norm, w_ple_gate, w_ple_proj, final_norm):
    batch, seq, d = x_prompt.shape
    nb, dec_seq, _ = x_sample.shape
    assert dec_seq == 1
    depth = norm_mix.shape[0]
    mp, ms = batch * seq, nb * dec_seq
    yp = x_prompt.reshape(mp, d)
    ys = x_sample.reshape(ms, d)
    row = lambda a: a.reshape(1, -1)
    fn = row(final_norm)

    ret_p, ret_s, kp_rows, vp_rows, ks_rows, vs_rows = [], [], [], [], [], []
    for i in range(depth):
        g_mix = row(norm_mix[i])
        if i % N_MIXERS == 0:
            r = i // N_MIXERS
            n_heads = RET_HEADS
            dk = d // n_heads
            dv = 2 * dk
            w_in = ret_w_in[r].astype(BF16)
            w_out = ret_w_out[r].astype(BF16)
            k_scale = dk ** -0.5
            w_in_p = jnp.concatenate([_split_even_odd_columns(w_in[:, :2 * d], dk), w_in[:, 2 * d:]], axis=1)
            cos_p, sin_p = _rotary_half_tables(seq, dk)
            proj_p = _ret_proj(yp, g_mix, w_in_p, cos_p, sin_p, k_scale, dk, True)
            dec_in, dec_q, dec_k = _decay_tables(n_heads, RET_CHUNK)
            og_p, s_p = _ret_scan(proj_p, batch, seq, dec_in, dec_q, dec_k)
            yp = _matmul_res(og_p, w_out, yp)
            ret_p.append(_merge_even_odd_rows(s_p, dk).astype(x_prompt.dtype))
            cos_s, sin_s = _rotary_tables(PAST_LEN, 1, dk, rows=ms)
            proj_s = _ret_proj(ys, g_mix, w_in, cos_s, sin_s, k_scale, dk, False).astype(F32)
            gammas = tuple(math.exp(math.log1p(-2.0 ** (-5.0 - h))) for h in range(n_heads))
            q_col = proj_s[:, :d].reshape(ms, n_heads, 1, dk)
            k_col = proj_s[:, d:2 * d].reshape(ms, n_heads, 1, dk)
            v_row = proj_s[:, 2 * d:2 * d + n_heads * dv].reshape(ms, n_heads, 1, dv)
            g_row = proj_s[:, 2 * d + n_heads * dv:].reshape(ms, n_heads, 1, dv)
            og_s, s_s = _ret_step(q_col, k_col, v_row, g_row, state_ret[r].astype(F32), gammas)
            ys = _matmul_res(og_s.reshape(ms, n_heads * dv), w_out, ys)
            ret_s.append(s_s.astype(state_ret.dtype))
        else:
            di = i // N_MIXERS
            n_heads = DIFF_HEADS
            dv = d // n_heads
            dh = dv // 2
            lam_init = 0.8 - 0.6 * math.exp(-0.3 * i)
            w_in = diff_w_in[di].astype(BF16)
            w_out = diff_w_out[di].astype(BF16)
            lams = [row(a[di].astype(F32)) for a in (diff_lambda_q1, diff_lambda_k1, diff_lambda_q2, diff_lambda_k2)]
            sub = row(diff_subln[di].astype(F32))
            q_scale = dh ** -0.5 * math.log2(math.e)
            t = ATTN_BLOCK if seq % ATTN_BLOCK == 0 else seq
            q_p, kf_p, vf_p, kb_p, vt_p = _diff_proj(yp, g_mix, w_in, q_scale, t // 2, True)
            q_s, kf_s, vf_s = _diff_proj(ys, g_mix, w_in, q_scale, ms, False)
            n_phys, page_size = cache_k.shape[1], cache_k.shape[2]
            ck = cache_k[di].reshape(n_phys, page_size * n_heads, dv)
            cv = cache_v[di].reshape(n_phys, page_size * n_heads, dv)
            o_p, o_s = _diff_attn(q_p, kb_p, vt_p, batch, seq, t,
                                  q_s.astype(F32).reshape(ms, n_heads, dv), kf_s.reshape(ms, n_heads, dv),
                                  vf_s.reshape(ms, n_heads, dv), ck, cv, page_table, *lams, sub, lam_init)
            yp = _matmul_res(o_p, w_out, yp)
            kp_rows.append(kf_p.reshape(batch, seq, n_heads, dv))
            vp_rows.append(vf_p.reshape(batch, seq, n_heads, dv))
            ys = _matmul_res(o_s.reshape(ms, d), w_out, ys)
            ks_rows.append(kf_s.reshape(nb, dec_seq, n_heads, dv))
            vs_rows.append(vf_s.reshape(nb, dec_seq, n_heads, dv))
        g_ffn = row(norm_ffn[i])
        wu, wd = w_up[i].astype(BF16), w_down[i].astype(BF16)
        yp = _ffn(yp, g_ffn, wu, wd)
        ys = _ffn(ys, g_ffn, wu, wd)
        g_ple = row(ple_norm[i])
        wg, wp = w_ple_gate[i].astype(BF16), w_ple_proj[i].astype(BF16)
        last = i == depth - 1
        yp = _ple(yp, p_prompt.reshape(depth, mp, -1), i, g_ple, wg, wp, fn, last)
        ys = _ple(ys, p_sample.reshape(depth, ms, -1), i, g_ple, wg, wp, fn, last)

    y_prompt = yp.reshape(batch, seq, d)
    y_sample = ys.reshape(nb, dec_seq, d)
    return (y_prompt, y_sample, jnp.stack(ret_p), jnp.stack(ret_s),
            jnp.stack(kp_rows), jnp.stack(vp_rows), jnp.stack(ks_rows), jnp.stack(vs_rows))
```

```python
import functools
import math

import jax
import jax.numpy as jnp
from jax import lax
from jax.experimental import pallas as pl
from jax.experimental.pallas import tpu as pltpu

EPS = 1e-6
NEG_INF = -1e30
N_MIXERS = 2
RET_HEADS = 4
RET_CHUNK = 256
ROPE_BASE = 10000.0
DIFF_HEADS = 8
PAST_LEN = 8192

F32 = jnp.float32
BF16 = jnp.bfloat16
BF16_SUBLANES = 16
VMEM_LIMIT_BYTES = 48 * 1024 * 1024
ROW_TILE = 1024
ATTN_BLOCK = 512
PAGES_PER_STEP = 8
RET_CHUNKS_PER_STEP = 4
RET_PROJ_BLOCK = 1024
RET_STEP_ROWS = 2
COLUMN_TILE = 128
FFN_BLOCK = 1024


def _cparams(*sem):
    return pltpu.CompilerParams(dimension_semantics=sem, vmem_limit_bytes=VMEM_LIMIT_BYTES)


def _row_tile(m, pref=None):
    pref = ROW_TILE if pref is None else pref
    return pref if m % pref == 0 else m


def _rms(x, g):
    return x * lax.rsqrt(jnp.mean(x * x, axis=-1, keepdims=True) + EPS) * g


def _dot(a, b):
    return jnp.dot(a, b, preferred_element_type=F32)


def _dot_nt(a, b):
    return lax.dot_general(a, b, (((1,), (1,)), ((), ())), preferred_element_type=F32)


def _dot_tn(a, b):
    return lax.dot_general(a, b, (((0,), (0,)), ((), ())), preferred_element_type=F32)


def _ret_proj_kernel(x_ref, g_ref, wqk_ref, w_ref, cos_ref, sin_ref, o_ref, h_ref, *, n_qk_blocks, k_scale, dk,
                     split_halves):
    j = pl.program_id(1)

    @pl.when(j == 0)
    def _():
        h_ref[...] = _rms(x_ref[...], g_ref[...]).astype(BF16)

    @pl.when(j < n_qk_blocks)
    def _():
        acc = _dot(h_ref[...], wqk_ref[...])
        cos_t, sin_t = cos_ref[...], sin_ref[...]
        scale = jnp.where(j >= n_qk_blocks // 2, k_scale, 1.0).astype(F32)
        for hh in range(acc.shape[1] // dk):
            a = acc[:, hh * dk:(hh + 1) * dk]
            if split_halves:
                half = dk // 2
                ev, od = a[:, :half], a[:, half:]
                o_ref[:, hh * dk:hh * dk + half] = ((ev * cos_t - od * sin_t) * scale).astype(BF16)
                o_ref[:, hh * dk + half:(hh + 1) * dk] = ((od * cos_t + ev * sin_t) * scale).astype(BF16)
            else:
                even = (lax.broadcasted_iota(jnp.int32, a.shape, 1) & 1) == 0
                swapped = jnp.where(even, pltpu.roll(a, dk - 1, 1), pltpu.roll(a, 1, 1))
                o_ref[:, hh * dk:(hh + 1) * dk] = ((a * cos_t + swapped * sin_t) * scale).astype(BF16)

    @pl.when(j >= n_qk_blocks)
    def _():
        o_ref[...] = _dot(h_ref[...], w_ref[...]).astype(BF16)


def _ret_proj(x, g, w_qk, w, cos_t, sin_t, k_scale, dk, split_halves):
    m, d = x.shape
    n = w.shape[1]
    rows_per_seq, tw = cos_t.shape
    assert tw == (dk // 2 if split_halves else dk)
    tm = _row_tile(rows_per_seq)
    n_pos_blocks = rows_per_seq // tm
    tn = RET_PROJ_BLOCK if d % RET_PROJ_BLOCK == 0 else dk
    n_qk_blocks = 2 * d // tn
    kern = functools.partial(_ret_proj_kernel, n_qk_blocks=n_qk_blocks, k_scale=k_scale, dk=dk,
                             split_halves=split_halves)
    return pl.pallas_call(
        kern,
        grid=(m // tm, n // tn),
        in_specs=[
            pl.BlockSpec((tm, d), lambda i, j: (i, 0)),
            pl.BlockSpec((1, d), lambda i, j: (0, 0)),
            pl.BlockSpec((d, tn), lambda i, j: (0, jnp.minimum(j, n_qk_blocks - 1))),
            pl.BlockSpec((d, tn), lambda i, j: (0, jnp.maximum(j, n_qk_blocks))),
            pl.BlockSpec((tm, tw), lambda i, j: (i % n_pos_blocks, 0)),
            pl.BlockSpec((tm, tw), lambda i, j: (i % n_pos_blocks, 0)),
        ],
        out_specs=pl.BlockSpec((tm, tn), lambda i, j: (i, j)),
        out_shape=jax.ShapeDtypeStruct((m, n), BF16),
        scratch_shapes=[pltpu.VMEM((tm, d), BF16)],
        compiler_params=_cparams("parallel", "arbitrary"),
        name="ret_proj",
    )(x, g, w_qk, w, cos_t, sin_t)


def _ret_scan_kernel(q_ref, k_ref, v_ref, g_ref, din_ref, dq_ref, dk_ref, o_ref, sfin_ref, s_ref):
    c = pl.program_id(1)

    @pl.when(c == 0)
    def _():
        s_ref[...] = jnp.zeros_like(s_ref)

    chunk = din_ref.shape[1]
    dec_chunk = dq_ref[0, chunk - 1:chunk, :]
    for u in range(q_ref.shape[0] // chunk):
        rows = slice(u * chunk, (u + 1) * chunk)
        q, k, v = q_ref[rows, :], k_ref[rows, :], v_ref[rows, :]
        s_old = s_ref[...]
        scores = _dot_nt(q, k) * din_ref[0]
        o = _dot(scores.astype(BF16), v) + _dot(q, s_old.astype(BF16)) * dq_ref[0]
        k_dec = (k.astype(F32) * dk_ref[0]).astype(BF16)
        s_ref[...] = s_old * dec_chunk + _dot_tn(k_dec, v)
        mu = jnp.mean(o, axis=-1, keepdims=True)
        d = o - mu
        on = d * lax.rsqrt(jnp.mean(d * d, axis=-1, keepdims=True) + EPS)
        gate = g_ref[rows, :].astype(F32)
        o_ref[rows, :] = (gate * jax.nn.sigmoid(gate) * on).astype(BF16)

    @pl.when(c == pl.num_programs(1) - 1)
    def _():
        sfin_ref[0, 0] = s_ref[...]


def _ret_scan(proj, batch, seq, dec_in, dec_q, dec_k):
    n_heads = RET_HEADS
    m = proj.shape[0]
    dk = proj.shape[1] // (6 * n_heads)
    dv = 2 * dk
    assert seq % RET_CHUNK == 0
    chunks_per_step = RET_CHUNKS_PER_STEP if (seq // RET_CHUNK) % RET_CHUNKS_PER_STEP == 0 else 1
    c = RET_CHUNK * chunks_per_step
    nc = seq // c
    kern = _ret_scan_kernel
    row = lambda bh, ci: (bh // n_heads) * nc + ci
    return pl.pallas_call(
        kern,
        grid=(batch * n_heads, nc),
        in_specs=[
            pl.BlockSpec((c, dk), lambda bh, ci: (row(bh, ci), bh % n_heads)),
            pl.BlockSpec((c, dk), lambda bh, ci: (row(bh, ci), n_heads + bh % n_heads)),
            pl.BlockSpec((c, dv), lambda bh, ci: (row(bh, ci), n_heads + bh % n_heads)),
            pl.BlockSpec((c, dv), lambda bh, ci: (row(bh, ci), 2 * n_heads + bh % n_heads)),
            pl.BlockSpec((1, RET_CHUNK, RET_CHUNK), lambda bh, ci: (bh % n_heads, 0, 0)),
            pl.BlockSpec((1, RET_CHUNK, 1), lambda bh, ci: (bh % n_heads, 0, 0)),
            pl.BlockSpec((1, RET_CHUNK, 1), lambda bh, ci: (bh % n_heads, 0, 0)),
        ],
        out_specs=[
            pl.BlockSpec((c, dv), lambda bh, ci: (row(bh, ci), bh % n_heads)),
            pl.BlockSpec((1, 1, dk, dv), lambda bh, ci: (bh // n_heads, bh % n_heads, 0, 0)),
        ],
        out_shape=[
            jax.ShapeDtypeStruct((m, n_heads * dv), BF16),
            jax.ShapeDtypeStruct((batch, n_heads, dk, dv), F32),
        ],
        scratch_shapes=[pltpu.VMEM((dk, dv), F32)],
        compiler_params=_cparams("parallel", "arbitrary"),
        name="ret_scan",
    )(proj, proj, proj, proj, dec_in, dec_q, dec_k)


def _ret_step_kernel(q_ref, k_ref, v_ref, g_ref, s_ref, o_ref, snew_ref, *, gammas):
    dk, dv = s_ref.shape[2], s_ref.shape[3]

    def column(row):
        tile = jnp.broadcast_to(row, (COLUMN_TILE, dk)).T
        return jnp.concatenate([tile] * (dv // COLUMN_TILE), axis=1)

    for r in range(s_ref.shape[0]):
        for h, gamma in enumerate(gammas):
            q_row, k_row = q_ref[r, h], k_ref[r, h]
            q, k = column(q_row), column(k_row)
            v = v_ref[r, h]
            s_old = s_ref[r, h]
            qk = jnp.sum(q_row * k_row, axis=-1, keepdims=True)
            o = qk * v + gamma * jnp.sum(q * s_old, axis=0, keepdims=True)
            snew_ref[r, h] = s_old * gamma + k * v
            mu = jnp.mean(o, axis=-1, keepdims=True)
            d = o - mu
            on = d * lax.rsqrt(jnp.mean(d * d, axis=-1, keepdims=True) + EPS)
            gate = g_ref[r, h]
            o_ref[r, h] = (gate * jax.nn.sigmoid(gate) * on).astype(BF16)


def _ret_step(q_col, k_col, v_row, g_row, state, gammas):
    nb, n_heads, dk, dv = state.shape
    rb = RET_STEP_ROWS if nb % RET_STEP_ROWS == 0 else 1
    idx = lambda b: (b, 0, 0, 0)
    kern = functools.partial(_ret_step_kernel, gammas=gammas)
    return pl.pallas_call(
        kern,
        grid=(nb // rb,),
        in_specs=[
            pl.BlockSpec((rb, n_heads, 1, dk), idx),
            pl.BlockSpec((rb, n_heads, 1, dk), idx),
            pl.BlockSpec((rb, n_heads, 1, dv), idx),
            pl.BlockSpec((rb, n_heads, 1, dv), idx),
            pl.BlockSpec((rb, n_heads, dk, dv), idx),
        ],
        out_specs=[
            pl.BlockSpec((rb, n_heads, 1, dv), idx),
            pl.BlockSpec((rb, n_heads, dk, dv), idx),
        ],
        out_shape=[
            jax.ShapeDtypeStruct((nb, n_heads, 1, dv), BF16),
            jax.ShapeDtypeStruct(state.shape, F32),
        ],
        compiler_params=_cparams("parallel"),
        name="ret_step",
    )(q_col, k_col, v_row, g_row, state)


def _matmul_res_kernel(a_ref, w_ref, y_ref, o_ref):
    o_ref[...] = y_ref[...] + _dot(a_ref[...], w_ref[...])


def _matmul_res(a, w, y):
    m, k = a.shape
    n = w.shape[1]
    tm = _row_tile(m, 512)
    return pl.pallas_call(
        _matmul_res_kernel,
        grid=(m // tm,),
        in_specs=[
            pl.BlockSpec((tm, k), lambda i: (i, 0)),
            pl.BlockSpec((k, n), lambda i: (0, 0)),
            pl.BlockSpec((tm, n), lambda i: (i, 0)),
        ],
        out_specs=pl.BlockSpec((tm, n), lambda i: (i, 0)),
        out_shape=jax.ShapeDtypeStruct((m, n), F32),
        compiler_params=_cparams("parallel"),
        name="mix_out",
    )(a, w, y)


def _ffn_kernel(y_ref, g_ref, wu_ref, wd_ref, o_ref, h_ref, acc_ref):
    j = pl.program_id(1)

    @pl.when(j == 0)
    def _():
        h_ref[...] = _rms(y_ref[...], g_ref[...]).astype(BF16)
        acc_ref[...] = jnp.zeros_like(acc_ref)

    u = jnp.maximum(_dot(h_ref[...], wu_ref[...]), 0.0)
    acc_ref[...] += _dot((u * u).astype(BF16), wd_ref[...])

    @pl.when(j == pl.num_programs(1) - 1)
    def _():
        o_ref[...] = y_ref[...] + acc_ref[...]


def _ffn(y, g, w_up, w_down):
    m, d = y.shape
    f = w_up.shape[1]
    tm, tf = _row_tile(m), FFN_BLOCK
    return pl.pallas_call(
        _ffn_kernel,
        grid=(m // tm, f // tf),
        in_specs=[
            pl.BlockSpec((tm, d), lambda i, j: (i, 0)),
            pl.BlockSpec((1, d), lambda i, j: (0, 0)),
            pl.BlockSpec((d, tf), lambda i, j: (0, j)),
            pl.BlockSpec((tf, d), lambda i, j: (j, 0)),
        ],
        out_specs=pl.BlockSpec((tm, d), lambda i, j: (i, 0)),
        out_shape=jax.ShapeDtypeStruct((m, d), F32),
        scratch_shapes=[pltpu.VMEM((tm, d), BF16), pltpu.VMEM((tm, d), F32)],
        compiler_params=_cparams("parallel", "arbitrary"),
        name="ffn",
    )(y, g, w_up, w_down)


def _ple_kernel(y_ref, p_ref, g_ref, wg_ref, wp_ref, fn_ref, o_ref, *, final_norm):
    y = y_ref[...]
    gate = jax.nn.sigmoid(_dot(_rms(y, g_ref[...]).astype(BF16), wg_ref[...]))
    out = y + gate * _dot(p_ref[0].astype(BF16), wp_ref[...])
    if final_norm:
        out = _rms(out, fn_ref[...])
    o_ref[...] = out


def _ple(y, p_layers, layer, g, w_gate, w_proj, fn, final_norm):
    m, d = y.shape
    dp = p_layers.shape[2]
    tm = _row_tile(m, 512)
    kern = functools.partial(_ple_kernel, final_norm=final_norm)
    return pl.pallas_call(
        kern,
        grid=(m // tm,),
        in_specs=[
            pl.BlockSpec((tm, d), lambda i: (i, 0)),
            pl.BlockSpec((1, tm, dp), lambda i: (layer, i, 0)),
            pl.BlockSpec((1, d), lambda i: (0, 0)),
            pl.BlockSpec((d, d), lambda i: (0, 0)),
            pl.BlockSpec((dp, d), lambda i: (0, 0)),
            pl.BlockSpec((1, d), lambda i: (0, 0)),
        ],
        out_specs=pl.BlockSpec((tm, d), lambda i: (i, 0)),
        out_shape=jax.ShapeDtypeStruct((m, d), F32),
        compiler_params=_cparams("parallel"),
        name="ple",
    )(y, p_layers, g, w_gate, w_proj, fn)


def _diff_proj_kernel(x_ref, g_ref, w_ref, q_ref, kf_ref, vf_ref, *rest, q_scale, with_bf16_kv):
    h = _rms(x_ref[...], g_ref[...]).astype(BF16)
    d = h.shape[1]
    q_ref[...] = (_dot(h, w_ref[:, :d]) * q_scale).astype(BF16)
    k = _dot(h, w_ref[:, d:2 * d])
    kf_ref[...] = k
    v = _dot(h, w_ref[:, 2 * d:])
    vf_ref[...] = v
    if with_bf16_kv:
        kb_ref, vt_ref = rest
        kb_ref[...] = k.astype(BF16)
        vt_ref[0] = v.T.astype(BF16)


def _diff_proj(x, g, w, q_scale, tm, with_bf16_kv):
    m, d = x.shape
    kern = functools.partial(_diff_proj_kernel, q_scale=q_scale, with_bf16_kv=with_bf16_kv)
    out_spec = pl.BlockSpec((tm, d), lambda i: (i, 0))
    out_specs = [out_spec] * 3
    out_shape = [
        jax.ShapeDtypeStruct((m, d), BF16),
        jax.ShapeDtypeStruct((m, d), F32),
        jax.ShapeDtypeStruct((m, d), F32),
    ]
    if with_bf16_kv:
        out_specs += [out_spec, pl.BlockSpec((1, d, tm), lambda i: (i, 0, 0))]
        out_shape += [jax.ShapeDtypeStruct((m, d), BF16), jax.ShapeDtypeStruct((m // tm, d, tm), BF16)]
    return pl.pallas_call(
        kern,
        grid=(m // tm,),
        in_specs=[
            pl.BlockSpec((tm, d), lambda i: (i, 0)),
            pl.BlockSpec((1, d), lambda i: (0, 0)),
            pl.BlockSpec((d, 3 * d), lambda i: (0, 0)),
        ],
        out_specs=out_specs,
        out_shape=out_shape,
        compiler_params=_cparams("parallel"),
        name="diff_proj",
    )(x, g, w)


def _lambda_value(lq1, lk1, lq2, lk2, lam_init):
    s1 = jnp.sum(lq1 * lk1, axis=-1, keepdims=True)
    s2 = jnp.sum(lq2 * lk2, axis=-1, keepdims=True)
    return jnp.exp(s1) - jnp.exp(s2) + lam_init


def _subln(o, g, lam_init):
    return _rms(o, g) * (1.0 - lam_init)


def _diff_attn_kernel(pt_ref, q_ref, k_ref, vt_ref, lq1_ref, lk1_ref, lq2_ref, lk2_ref, sub_ref,
                      subrow_ref, qs_ref, kn_ref, vn_ref, *refs, lam_init, dh, n_pages_step, n_groups,
                      n_decode_steps, n_attn_steps, n_q_blocks):
    k_pages = refs[:n_pages_step]
    v_pages = refs[n_pages_step:2 * n_pages_step]
    (o_ref, os_ref, m_ref, acc_ref, sa_ref, sb_ref, mxa_ref, mxb_ref,
     dm_ref, dl_ref, dacc_ref) = refs[2 * n_pages_step:]
    qi = pl.program_id(1)
    step = pl.program_id(0) * n_q_blocks + qi

    @pl.when(step == 0)
    def _():
        dm_ref[...] = jnp.full_like(dm_ref, NEG_INF)
        dl_ref[...] = jnp.zeros_like(dl_ref)
        dacc_ref[...] = jnp.zeros_like(dacc_ref)

    tk = k_ref.shape[1]
    dv = vt_ref.shape[1]
    q = q_ref[...]
    m_ref[...] = jnp.full_like(m_ref, NEG_INF)
    acc_ref[...] = jnp.zeros_like(acc_ref)
    ones_rows = jnp.ones((acc_ref.shape[1] - dv, tk), BF16)

    def scores(ki, s_ref, mx_ref):
        k = k_ref[ki]
        for i in range(2):
            s = _dot_nt(k[:, i * dh:(i + 1) * dh], q[:, i * dh:(i + 1) * dh])
            s_ref[i] = s
            mx_ref[i] = jnp.max(s, axis=0, keepdims=True)

    def update(ki, s_ref, mx_ref, key_offset=None):
        vt = jnp.concatenate([vt_ref[ki], ones_rows], axis=0)
        for i in range(2):
            s = s_ref[i]
            if key_offset is None:
                mx = mx_ref[i]
            else:
                key = lax.broadcasted_iota(jnp.int32, s.shape, 0) + key_offset
                qry = lax.broadcasted_iota(jnp.int32, s.shape, 1)
                s = jnp.where(key <= qry, s, NEG_INF)
                mx = jnp.max(s, axis=0, keepdims=True)
            m_old = m_ref[i]
            m_new = jnp.maximum(m_old, mx)
            alpha = jnp.exp2(m_old - m_new)
            pr = jnp.exp2(s - m_new)
            acc_ref[i] = alpha * acc_ref[i] + _dot(vt, pr.astype(BF16))
            m_ref[i] = m_new

    scores(0, sa_ref, mxa_ref)

    def pair(j):
        scores(2 * j + 1, sb_ref, mxb_ref)
        update(2 * j, sa_ref, mxa_ref)
        scores(2 * j + 2, sa_ref, mxa_ref)
        update(2 * j + 1, sb_ref, mxb_ref)

    def body(jj, carry):
        pair(2 * jj)
        pair(2 * jj + 1)
        return carry

    lax.fori_loop(0, qi >> 1, body, 0)
    pl.when((qi & 1) == 1)(lambda: pair(qi - 1))
    scores(2 * qi + 1, sb_ref, mxb_ref)
    update(2 * qi, sa_ref, mxa_ref, 0)
    update(2 * qi + 1, sb_ref, mxb_ref, tk)

    lam = _lambda_value(lq1_ref[...], lk1_ref[...], lq2_ref[...], lk2_ref[...], lam_init)
    a1, a2 = acc_ref[0], acc_ref[1]
    r1, r2 = 1.0 / a1[dv:dv + 1], 1.0 / a2[dv:dv + 1]
    o = a1[:dv] * r1 - lam * (a2[:dv] * r2)
    ms = jnp.mean(o * o, axis=0, keepdims=True)
    o = o * lax.rsqrt(ms + EPS) * sub_ref[...] * (1.0 - lam_init)
    o_ref[...] = o.T.astype(BF16)

    def decode():
        group = step % n_groups
        _decode_update(group == 0, group == n_groups - 1, lam, qs_ref, kn_ref, vn_ref, subrow_ref, k_pages, v_pages,
                       os_ref, dm_ref, dl_ref, dacc_ref, lam_init=lam_init, dh=dh)

    if n_decode_steps == n_attn_steps:
        decode()
    else:
        pl.when(step < n_decode_steps)(decode)


def _diff_attn(q, k, vt, batch, seq, t, q_dec, k_new, v_new, cache_k, cache_v, page_table,
               lq1, lk1, lq2, lk2, subln_row, lam_init):
    n_heads = DIFF_HEADS
    assert n_heads & (n_heads - 1) == 0
    m, d = q.shape
    dv = d // n_heads
    dh = dv // 2
    nb = seq // t
    tk = t // 2
    nkb = seq // tk
    assert vt.shape == (m // tk, d, tk)
    n_attn_steps = batch * n_heads * nb
    n_dec, n_pages = page_table.shape
    pps = max(p for p in range(1, PAGES_PER_STEP + 1) if n_pages % p == 0)
    n_groups = n_pages // pps
    n_decode_steps = n_dec * n_groups
    assert n_decode_steps <= n_attn_steps, "decode page groups must fit in the prompt attention grid"
    rows = cache_k.shape[1]
    kern = functools.partial(_diff_attn_kernel, lam_init=lam_init, dh=dh, n_pages_step=pps, n_groups=n_groups,
                             n_decode_steps=n_decode_steps, n_attn_steps=n_attn_steps, n_q_blocks=nb)
    vec = lambda n: pl.BlockSpec((1, n), lambda bh, qi, pt: (0, 0))
    q_map = lambda bh, qi, pt: ((bh // n_heads) * nb + qi, bh % n_heads)
    dstep = lambda bh, qi: jnp.minimum(bh * nb + qi, n_decode_steps - 1)
    tok = pl.BlockSpec((1, n_heads, dv), lambda bh, qi, pt: (dstep(bh, qi) // n_groups, 0, 0))

    def page(u):
        def index(bh, qi, pt):
            s = dstep(bh, qi)
            return (pt[s // n_groups, (s % n_groups) * pps + u], 0, 0)
        return pl.BlockSpec((1, rows, dv), index)

    return pl.pallas_call(
        kern,
        grid_spec=pltpu.PrefetchScalarGridSpec(
            num_scalar_prefetch=1,
            grid=(batch * n_heads, nb),
            in_specs=[
                pl.BlockSpec((t, dv), q_map),
                pl.BlockSpec((nkb, tk, dv), lambda bh, qi, pt: (bh // n_heads, 0, bh % n_heads)),
                pl.BlockSpec((nkb, dv, tk), lambda bh, qi, pt: (bh // n_heads, bh % n_heads, 0)),
                vec(dh), vec(dh), vec(dh), vec(dh),
                pl.BlockSpec((dv, 1), lambda bh, qi, pt: (0, 0)),
                vec(dv), tok, tok, tok,
            ] + [page(u) for u in range(pps)] + [page(u) for u in range(pps)],
            out_specs=[pl.BlockSpec((t, dv), q_map), tok],
            scratch_shapes=[
                pltpu.VMEM((2, 1, t), F32),
                pltpu.VMEM((2, dv + BF16_SUBLANES, t), F32),
                pltpu.VMEM((2, tk, t), F32),
                pltpu.VMEM((2, tk, t), F32),
                pltpu.VMEM((2, 1, t), F32),
                pltpu.VMEM((2, 1, t), F32),
                pltpu.VMEM((2 * n_heads, 1), F32),
                pltpu.VMEM((2 * n_heads, 1), F32),
                pltpu.VMEM((2 * n_heads, dv), F32),
            ],
        ),
        out_shape=[jax.ShapeDtypeStruct((m, d), BF16), jax.ShapeDtypeStruct((n_dec, n_heads, dv), BF16)],
        compiler_params=_cparams("arbitrary", "arbitrary"),
        name="diff_attn",
    )(page_table, q, k.reshape(m // tk, tk, d), vt, lq1, lk1, lq2, lk2, subln_row.reshape(dv, 1),
      subln_row, q_dec, k_new, v_new, *([cache_k] * pps), *([cache_v] * pps))


def _decode_update(first_group, last_group, lam, q_ref, kn_ref, vn_ref, sub_ref, k_refs, v_refs,
                   o_ref, m_ref, l_ref, acc_ref, *, lam_init, dh):
    n_heads = q_ref.shape[1]
    n_pages_step = len(k_refs)
    q = q_ref[0]
    lane = lax.broadcasted_iota(jnp.int32, q.shape, 1)
    first = lane < dh
    qm = jnp.concatenate([jnp.where(first, q, 0.0), jnp.where(first, 0.0, q)], axis=0).astype(BF16)
    s = jnp.concatenate([_dot_nt(qm, kr[0].astype(BF16)) for kr in k_refs], axis=1)
    head_mask = n_heads - 1
    own = ((lax.broadcasted_iota(jnp.int32, s.shape, 1) & head_mask)
           == (lax.broadcasted_iota(jnp.int32, s.shape, 0) & head_mask))
    s = jnp.where(own, s, NEG_INF)
    m_old = jnp.where(first_group, NEG_INF, m_ref[...])
    l_old = jnp.where(first_group, 0.0, l_ref[...])
    acc_old = jnp.where(first_group, 0.0, acc_ref[...])
    m_new = jnp.maximum(m_old, jnp.max(s, axis=-1, keepdims=True))
    alpha = jnp.exp2(m_old - m_new)
    pr = jnp.exp2(s - m_new)
    l_new = alpha * l_old + jnp.sum(pr, axis=-1, keepdims=True)
    pb = pr.astype(BF16)
    rows = k_refs[0].shape[1]
    pv = _dot(pb[:, :rows], v_refs[0][0].astype(BF16))
    for u in range(1, n_pages_step):
        pv += _dot(pb[:, u * rows:(u + 1) * rows], v_refs[u][0].astype(BF16))
    acc_new = alpha * acc_old + pv
    m_ref[...] = m_new
    l_ref[...] = l_new
    acc_ref[...] = acc_new

    @pl.when(last_group)
    def _():
        kn, vn = kn_ref[0], vn_ref[0]
        prod = q * kn
        sn = jnp.concatenate([jnp.sum(jnp.where(first, prod, 0.0), axis=-1, keepdims=True),
                              jnp.sum(jnp.where(first, 0.0, prod), axis=-1, keepdims=True)], axis=0)
        m_fin = jnp.maximum(m_new, sn)
        a2 = jnp.exp2(m_new - m_fin)
        pn = jnp.exp2(sn - m_fin)
        l_fin = a2 * l_new + pn
        acc_fin = a2 * acc_new + pn * jnp.concatenate([vn, vn], axis=0)
        o2 = acc_fin / l_fin
        o = o2[:n_heads] - lam * o2[n_heads:]
        o_ref[0] = _subln(o, sub_ref[...], lam_init).astype(BF16)


def _rotary_tables(start, length, dk, rows=None):
    angle = 1.0 / (ROPE_BASE ** jnp.linspace(0.0, 1.0, dk // 2, dtype=F32))
    angle = jnp.repeat(angle, 2)
    pos = start + jnp.arange(length, dtype=F32)
    th = pos[:, None] * angle[None, :]
    sign = jnp.where(jnp.arange(dk) % 2 == 0, -1.0, 1.0).astype(F32)
    cos_t = jnp.cos(th)
    sin_t = jnp.sin(th) * sign[None, :]
    if rows is not None:
        cos_t = jnp.broadcast_to(cos_t, (rows, cos_t.shape[1]))
        sin_t = jnp.broadcast_to(sin_t, (rows, sin_t.shape[1]))
    return cos_t, sin_t


def _rotary_half_tables(length, dk, block=128):
    angle = 1.0 / (ROPE_BASE ** jnp.linspace(0.0, 1.0, dk // 2, dtype=F32))
    n_hi = -(-length // block)
    th_hi = (jnp.arange(n_hi, dtype=F32) * block)[:, None] * angle[None, :]
    th_lo = jnp.arange(block, dtype=F32)[:, None] * angle[None, :]
    ch, sh = jnp.cos(th_hi)[:, None, :], jnp.sin(th_hi)[:, None, :]
    cl, sl = jnp.cos(th_lo)[None, :, :], jnp.sin(th_lo)[None, :, :]
    cos_t = (ch * cl - sh * sl).reshape(n_hi * block, dk // 2)[:length]
    sin_t = (sh * cl + ch * sl).reshape(n_hi * block, dk // 2)[:length]
    return cos_t, sin_t


def _split_cols_kernel(w_ref, o_ref):
    dk = w_ref.shape[1]
    half = dk // 2
    src = lax.broadcasted_iota(jnp.int32, (dk, dk), 0)
    dst = lax.broadcasted_iota(jnp.int32, (dk, dk), 1)
    perm = (src == jnp.where(dst < half, 2 * dst, 2 * (dst - half) + 1)).astype(BF16)
    o_ref[...] = _dot(w_ref[...], perm).astype(BF16)


def _split_even_odd_columns(w, n_cols, dk):
    rows = w.shape[0]
    return pl.pallas_call(
        _split_cols_kernel,
        grid=(n_cols // dk,),
        in_specs=[pl.BlockSpec((rows, dk), lambda c: (0, c))],
        out_specs=pl.BlockSpec((rows, dk), lambda c: (0, c)),
        out_shape=jax.ShapeDtypeStruct((rows, n_cols), BF16),
        compiler_params=_cparams("parallel"),
        name="split_cols",
    )(w)


def _merge_even_odd_rows(s, dk):
    lead = s.shape[:-2]
    return s.reshape(*lead, 2, dk // 2, s.shape[-1]).swapaxes(-3, -2).reshape(*lead, dk, s.shape[-1])


def _decay_tables(n_heads, c):
    lg = jnp.log1p(-jnp.exp2(-5.0 - jnp.arange(n_heads, dtype=F32)))
    idx = jnp.arange(c, dtype=F32)
    rel = idx[:, None] - idx[None, :]
    dec_in = jnp.where(rel[None] >= 0, jnp.exp(lg[:, None, None] * jnp.maximum(rel, 0.0)[None]), 0.0)
    dec_q = jnp.exp(lg[:, None] * (idx + 1.0)[None])[:, :, None]
    dec_k = jnp.exp(lg[:, None] * (c - 1.0 - idx)[None])[:, :, None]
    return dec_in, dec_q, dec_k


def kernel(x_prompt, x_sample, state_ret, cache_k, cache_v, page_table, p_prompt, p_sample, norm_mix, ret_w_in, ret_w_out, diff_w_in, diff_w_out, diff_lambda_q1, diff_lambda_k1, diff_lambda_q2, diff_lambda_k2, diff_subln, norm_ffn, w_up, w_down, ple_norm, w_ple_gate, w_ple_proj, final_norm):
    batch, seq, d = x_prompt.shape
    nb, dec_seq, _ = x_sample.shape
    assert dec_seq == 1
    depth = norm_mix.shape[0]
    mp, ms = batch * seq, nb * dec_seq
    yp = x_prompt.reshape(mp, d)
    ys = x_sample.reshape(ms, d)
    row = lambda a: a.reshape(1, -1)
    fn = row(final_norm)

    ret_p, ret_s, kp_rows, vp_rows, ks_rows, vs_rows = [], [], [], [], [], []
    for i in range(depth):
        g_mix = row(norm_mix[i])
        if i % N_MIXERS == 0:
            r = i // N_MIXERS
            n_heads = RET_HEADS
            dk = d // n_heads
            dv = 2 * dk
            w_in = ret_w_in[r].astype(BF16)
            w_out = ret_w_out[r].astype(BF16)
            k_scale = dk ** -0.5
            w_qk_split = _split_even_odd_columns(w_in, 2 * d, dk)
            cos_p, sin_p = _rotary_half_tables(seq, dk)
            proj_p = _ret_proj(yp, g_mix, w_qk_split, w_in, cos_p, sin_p, k_scale, dk, True)
            dec_in, dec_q, dec_k = _decay_tables(n_heads, RET_CHUNK)
            og_p, s_p = _ret_scan(proj_p, batch, seq, dec_in, dec_q, dec_k)
            yp = _matmul_res(og_p, w_out, yp)
            ret_p.append(_merge_even_odd_rows(s_p, dk).astype(x_prompt.dtype))
            cos_s, sin_s = _rotary_tables(PAST_LEN, 1, dk, rows=ms)
            proj_s = _ret_proj(ys, g_mix, w_in, w_in, cos_s, sin_s, k_scale, dk, False).astype(F32)
            gammas = tuple(math.exp(math.log1p(-2.0 ** (-5.0 - h))) for h in range(n_heads))
            q_col = proj_s[:, :d].reshape(ms, n_heads, 1, dk)
            k_col = proj_s[:, d:2 * d].reshape(ms, n_heads, 1, dk)
            v_row = proj_s[:, 2 * d:2 * d + n_heads * dv].reshape(ms, n_heads, 1, dv)
            g_row = proj_s[:, 2 * d + n_heads * dv:].reshape(ms, n_heads, 1, dv)
            og_s, s_s = _ret_step(q_col, k_col, v_row, g_row, state_ret[r].astype(F32), gammas)
            ys = _matmul_res(og_s.reshape(ms, n_heads * dv), w_out, ys)
            ret_s.append(s_s.astype(state_ret.dtype))
        else:
            di = i // N_MIXERS
            n_heads = DIFF_HEADS
            dv = d // n_heads
            dh = dv // 2
            lam_init = 0.8 - 0.6 * math.exp(-0.3 * i)
            w_in = diff_w_in[di].astype(BF16)
            w_out = diff_w_out[di].astype(BF16)
            lams = [row(a[di].astype(F32)) for a in (diff_lambda_q1, diff_lambda_k1, diff_lambda_q2, diff_lambda_k2)]
            sub = row(diff_subln[di].astype(F32))
            q_scale = dh ** -0.5 * math.log2(math.e)
            t = ATTN_BLOCK if seq % ATTN_BLOCK == 0 else seq
            q_p, kf_p, vf_p, kb_p, vt_p = _diff_proj(yp, g_mix, w_in, q_scale, t // 2, True)
            q_s, kf_s, vf_s = _diff_proj(ys, g_mix, w_in, q_scale, ms, False)
            n_phys, page_size = cache_k.shape[1], cache_k.shape[2]
            ck = cache_k[di].reshape(n_phys, page_size * n_heads, dv)
            cv = cache_v[di].reshape(n_phys, page_size * n_heads, dv)
            o_p, o_s = _diff_attn(q_p, kb_p, vt_p, batch, seq, t,
                                  q_s.astype(F32).reshape(ms, n_heads, dv), kf_s.reshape(ms, n_heads, dv),
                                  vf_s.reshape(ms, n_heads, dv), ck, cv, page_table, *lams, sub, lam_init)
            yp = _matmul_res(o_p, w_out, yp)
            kp_rows.append(kf_p.reshape(batch, seq, n_heads, dv))
            vp_rows.append(vf_p.reshape(batch, seq, n_heads, dv))
            ys = _matmul_res(o_s.reshape(ms, d), w_out, ys)
            ks_rows.append(kf_s.reshape(nb, dec_seq, n_heads, dv))
            vs_rows.append(vf_s.reshape(nb, dec_seq, n_heads, dv))
        g_ffn = row(norm_ffn[i])
        wu, wd = w_up[i].astype(BF16), w_down[i].astype(BF16)
        yp = _ffn(yp, g_ffn, wu, wd)
        ys = _ffn(ys, g_ffn, wu, wd)
        g_ple = row(ple_norm[i])
        wg, wp = w_ple_gate[i].astype(BF16), w_ple_proj[i].astype(BF16)
        last = i == depth - 1
        yp = _ple(yp, p_prompt.reshape(depth, mp, -1), i, g_ple, wg, wp, fn, last)
        ys = _ple(ys, p_sample.reshape(depth, ms, -1), i, g_ple, wg, wp, fn, last)

    y_prompt = yp.reshape(batch, seq, d)
    y_sample = ys.reshape(nb, dec_seq, d)
    return (y_prompt, y_sample, jnp.stack(ret_p), jnp.stack(ret_s),
            jnp.stack(kp_rows), jnp.stack(vp_rows), jnp.stack(ks_rows), jnp.stack(vs_rows))
```

```python
import functools
import math

import jax
import jax.numpy as jnp
from jax import lax
from jax.experimental import pallas as pl
from jax.experimental.pallas import tpu as pltpu

EPS = 1e-6
NEG_INF = -1e30
N_MIXERS = 2
RET_HEADS = 4
RET_CHUNK = 256
ROPE_BASE = 10000.0
DIFF_HEADS = 8
PAST_LEN = 8192

F32 = jnp.float32
BF16 = jnp.bfloat16
BF16_SUBLANES = 16
VMEM_LIMIT_BYTES = 48 * 1024 * 1024
ROW_TILE = 1024
ATTN_BLOCK = 512
PAGES_PER_STEP = 8
RET_CHUNKS_PER_STEP = 4
RET_PROJ_BLOCK = 1024
RET_STEP_ROWS = 2
COLUMN_TILE = 128
FFN_BLOCK = 1024


def _cparams(*sem):
    return pltpu.CompilerParams(dimension_semantics=sem, vmem_limit_bytes=VMEM_LIMIT_BYTES)


def _row_tile(m, pref=None):
    pref = ROW_TILE if pref is None else pref
    return pref if m % pref == 0 else m


def _rms(x, g):
    return x * lax.rsqrt(jnp.mean(x * x, axis=-1, keepdims=True) + EPS) * g


def _dot(a, b):
    return jnp.dot(a, b, preferred_element_type=F32)


def _dot_nt(a, b):
    return lax.dot_general(a, b, (((1,), (1,)), ((), ())), preferred_element_type=F32)


def _dot_tn(a, b):
    return lax.dot_general(a, b, (((0,), (0,)), ((), ())), preferred_element_type=F32)


def _ret_proj_kernel(x_ref, g_ref, wqk_ref, w_ref, cos_ref, sin_ref, o_ref, h_ref, *, n_qk_blocks, k_scale, dk,
                     split_halves):
    j = pl.program_id(1)

    @pl.when(j == 0)
    def _():
        h_ref[...] = _rms(x_ref[...], g_ref[...]).astype(BF16)

    @pl.when(j < n_qk_blocks)
    def _():
        acc = _dot(h_ref[...], wqk_ref[...])
        cos_t, sin_t = cos_ref[...], sin_ref[...]
        scale = jnp.where(j >= n_qk_blocks // 2, k_scale, 1.0).astype(F32)
        for hh in range(acc.shape[1] // dk):
            a = acc[:, hh * dk:(hh + 1) * dk]
            if split_halves:
                half = dk // 2
                ev, od = a[:, :half], a[:, half:]
                o_ref[:, hh * dk:hh * dk + half] = ((ev * cos_t - od * sin_t) * scale).astype(BF16)
                o_ref[:, hh * dk + half:(hh + 1) * dk] = ((od * cos_t + ev * sin_t) * scale).astype(BF16)
            else:
                even = (lax.broadcasted_iota(jnp.int32, a.shape, 1) & 1) == 0
                swapped = jnp.where(even, pltpu.roll(a, dk - 1, 1), pltpu.roll(a, 1, 1))
                o_ref[:, hh * dk:(hh + 1) * dk] = ((a * cos_t + swapped * sin_t) * scale).astype(BF16)

    @pl.when(j >= n_qk_blocks)
    def _():
        o_ref[...] = _dot(h_ref[...], w_ref[...]).astype(BF16)


def _ret_proj(x, g, w_qk, w, cos_t, sin_t, k_scale, dk, split_halves):
    m, d = x.shape
    n = w.shape[1]
    rows_per_seq, tw = cos_t.shape
    assert tw == (dk // 2 if split_halves else dk)
    tm = _row_tile(rows_per_seq)
    n_pos_blocks = rows_per_seq // tm
    tn = RET_PROJ_BLOCK if d % RET_PROJ_BLOCK == 0 else dk
    n_qk_blocks = 2 * d // tn
    kern = functools.partial(_ret_proj_kernel, n_qk_blocks=n_qk_blocks, k_scale=k_scale, dk=dk,
                             split_halves=split_halves)
    return pl.pallas_call(
        kern,
        grid=(m // tm, n // tn),
        in_specs=[
            pl.BlockSpec((tm, d), lambda i, j: (i, 0)),
            pl.BlockSpec((1, d), lambda i, j: (0, 0)),
            pl.BlockSpec((d, tn), lambda i, j: (0, jnp.minimum(j, n_qk_blocks - 1))),
            pl.BlockSpec((d, tn), lambda i, j: (0, jnp.maximum(j, n_qk_blocks))),
            pl.BlockSpec((tm, tw), lambda i, j: (i % n_pos_blocks, 0)),
            pl.BlockSpec((tm, tw), lambda i, j: (i % n_pos_blocks, 0)),
        ],
        out_specs=pl.BlockSpec((tm, tn), lambda i, j: (i, j)),
        out_shape=jax.ShapeDtypeStruct((m, n), BF16),
        scratch_shapes=[pltpu.VMEM((tm, d), BF16)],
        compiler_params=_cparams("parallel", "arbitrary"),
        name="ret_proj",
    )(x, g, w_qk, w, cos_t, sin_t)


def _ret_scan_kernel(q_ref, k_ref, v_ref, g_ref, din_ref, dq_ref, dk_ref, o_ref, sfin_ref, s_ref):
    c = pl.program_id(1)

    @pl.when(c == 0)
    def _():
        s_ref[...] = jnp.zeros_like(s_ref)

    chunk = din_ref.shape[1]
    dec_chunk = dq_ref[0, chunk - 1:chunk, :]
    for u in range(q_ref.shape[0] // chunk):
        rows = slice(u * chunk, (u + 1) * chunk)
        q, k, v = q_ref[rows, :], k_ref[rows, :], v_ref[rows, :]
        s_old = s_ref[...]
        scores = _dot_nt(q, k) * din_ref[0]
        o = _dot(scores.astype(BF16), v) + _dot(q, s_old.astype(BF16)) * dq_ref[0]
        k_dec = (k.astype(F32) * dk_ref[0]).astype(BF16)
        s_ref[...] = s_old * dec_chunk + _dot_tn(k_dec, v)
        mu = jnp.mean(o, axis=-1, keepdims=True)
        d = o - mu
        on = d * lax.rsqrt(jnp.mean(d * d, axis=-1, keepdims=True) + EPS)
        gate = g_ref[rows, :].astype(F32)
        o_ref[rows, :] = (gate * jax.nn.sigmoid(gate) * on).astype(BF16)

    @pl.when(c == pl.num_programs(1) - 1)
    def _():
        sfin_ref[0, 0] = s_ref[...]


def _ret_scan(proj, batch, seq, dec_in, dec_q, dec_k):
    n_heads = RET_HEADS
    m = proj.shape[0]
    dk = proj.shape[1] // (6 * n_heads)
    dv = 2 * dk
    assert seq % RET_CHUNK == 0
    chunks_per_step = RET_CHUNKS_PER_STEP if (seq // RET_CHUNK) % RET_CHUNKS_PER_STEP == 0 else 1
    c = RET_CHUNK * chunks_per_step
    nc = seq // c
    kern = _ret_scan_kernel
    row = lambda bh, ci: (bh // n_heads) * nc + ci
    return pl.pallas_call(
        kern,
        grid=(batch * n_heads, nc),
        in_specs=[
            pl.BlockSpec((c, dk), lambda bh, ci: (row(bh, ci), bh % n_heads)),
            pl.BlockSpec((c, dk), lambda bh, ci: (row(bh, ci), n_heads + bh % n_heads)),
            pl.BlockSpec((c, dv), lambda bh, ci: (row(bh, ci), n_heads + bh % n_heads)),
            pl.BlockSpec((c, dv), lambda bh, ci: (row(bh, ci), 2 * n_heads + bh % n_heads)),
            pl.BlockSpec((1, RET_CHUNK, RET_CHUNK), lambda bh, ci: (bh % n_heads, 0, 0)),
            pl.BlockSpec((1, RET_CHUNK, 1), lambda bh, ci: (bh % n_heads, 0, 0)),
            pl.BlockSpec((1, RET_CHUNK, 1), lambda bh, ci: (bh % n_heads, 0, 0)),
        ],
        out_specs=[
            pl.BlockSpec((c, dv), lambda bh, ci: (row(bh, ci), bh % n_heads)),
            pl.BlockSpec((1, 1, dk, dv), lambda bh, ci: (bh // n_heads, bh % n_heads, 0, 0)),
        ],
        out_shape=[
            jax.ShapeDtypeStruct((m, n_heads * dv), BF16),
            jax.ShapeDtypeStruct((batch, n_heads, dk, dv), F32),
        ],
        scratch_shapes=[pltpu.VMEM((dk, dv), F32)],
        compiler_params=_cparams("parallel", "arbitrary"),
        name="ret_scan",
    )(proj, proj, proj, proj, dec_in, dec_q, dec_k)


def _ret_step_kernel(q_ref, k_ref, v_ref, g_ref, s_ref, o_ref, snew_ref, *, gammas):
    dk, dv = s_ref.shape[2], s_ref.shape[3]

    def column(row):
        tile = jnp.broadcast_to(row, (COLUMN_TILE, dk)).T
        return jnp.concatenate([tile] * (dv // COLUMN_TILE), axis=1)

    for r in range(s_ref.shape[0]):
        for h, gamma in enumerate(gammas):
            q_row, k_row = q_ref[r, h], k_ref[r, h]
            q, k = column(q_row), column(k_row)
            v = v_ref[r, h]
            s_old = s_ref[r, h]
            qk = jnp.sum(q_row * k_row, axis=-1, keepdims=True)
            o = qk * v + gamma * jnp.sum(q * s_old, axis=0, keepdims=True)
            snew_ref[r, h] = s_old * gamma + k * v
            mu = jnp.mean(o, axis=-1, keepdims=True)
            d = o - mu
            on = d * lax.rsqrt(jnp.mean(d * d, axis=-1, keepdims=True) + EPS)
            gate = g_ref[r, h]
            o_ref[r, h] = (gate * jax.nn.sigmoid(gate) * on).astype(BF16)


def _ret_step(q_col, k_col, v_row, g_row, state, gammas):
    nb, n_heads, dk, dv = state.shape
    rb = RET_STEP_ROWS if nb % RET_STEP_ROWS == 0 else 1
    idx = lambda b: (b, 0, 0, 0)
    kern = functools.partial(_ret_step_kernel, gammas=gammas)
    return pl.pallas_call(
        kern,
        grid=(nb // rb,),
        in_specs=[
            pl.BlockSpec((rb, n_heads, 1, dk), idx),
            pl.BlockSpec((rb, n_heads, 1, dk), idx),
            pl.BlockSpec((rb, n_heads, 1, dv), idx),
            pl.BlockSpec((rb, n_heads, 1, dv), idx),
            pl.BlockSpec((rb, n_heads, dk, dv), idx),
        ],
        out_specs=[
            pl.BlockSpec((rb, n_heads, 1, dv), idx),
            pl.BlockSpec((rb, n_heads, dk, dv), idx),
        ],
        out_shape=[
            jax.ShapeDtypeStruct((nb, n_heads, 1, dv), BF16),
            jax.ShapeDtypeStruct(state.shape, F32),
        ],
        compiler_params=_cparams("parallel"),
        name="ret_step",
    )(q_col, k_col, v_row, g_row, state)


def _matmul_res_kernel(a_ref, w_ref, y_ref, o_ref):
    o_ref[...] = y_ref[...] + _dot(a_ref[...], w_ref[...])


def _matmul_res(a, w, y):
    m, k = a.shape
    n = w.shape[1]
    tm = _row_tile(m)
    return pl.pallas_call(
        _matmul_res_kernel,
        grid=(m // tm,),
        in_specs=[
            pl.BlockSpec((tm, k), lambda i: (i, 0)),
            pl.BlockSpec((k, n), lambda i: (0, 0)),
            pl.BlockSpec((tm, n), lambda i: (i, 0)),
        ],
        out_specs=pl.BlockSpec((tm, n), lambda i: (i, 0)),
        out_shape=jax.ShapeDtypeStruct((m, n), F32),
        compiler_params=_cparams("parallel"),
        name="mix_out",
    )(a, w, y)


def _ffn_kernel(y_ref, g_ref, wu_ref, wd_ref, o_ref, h_ref, acc_ref, *, n_blocks):
    j = pl.program_id(1)
    last = n_blocks - 1

    @pl.when(j == 0)
    def _():
        h_ref[...] = _rms(y_ref[...], g_ref[...]).astype(BF16)

    def block():
        u = jnp.maximum(_dot(h_ref[...], wu_ref[...]), 0.0)
        return _dot((u * u).astype(BF16), wd_ref[...])

    if n_blocks == 1:
        o_ref[...] = y_ref[...] + block()
    else:
        @pl.when(j == 0)
        def _():
            acc_ref[...] = block()

        @pl.when((j > 0) & (j < last))
        def _():
            acc_ref[...] += block()

        @pl.when(j == last)
        def _():
            o_ref[...] = y_ref[...] + acc_ref[...] + block()


def _ffn(y, g, w_up, w_down):
    m, d = y.shape
    f = w_up.shape[1]
    tm, tf = _row_tile(m), FFN_BLOCK
    return pl.pallas_call(
        functools.partial(_ffn_kernel, n_blocks=f // tf),
        grid=(m // tm, f // tf),
        in_specs=[
            pl.BlockSpec((tm, d), lambda i, j: (i, 0)),
            pl.BlockSpec((1, d), lambda i, j: (0, 0)),
            pl.BlockSpec((d, tf), lambda i, j: (0, j)),
            pl.BlockSpec((tf, d), lambda i, j: (j, 0)),
        ],
        out_specs=pl.BlockSpec((tm, d), lambda i, j: (i, 0)),
        out_shape=jax.ShapeDtypeStruct((m, d), F32),
        scratch_shapes=[pltpu.VMEM((tm, d), BF16), pltpu.VMEM((tm, d), F32)],
        compiler_params=_cparams("parallel", "arbitrary"),
        name="ffn",
    )(y, g, w_up, w_down)


def _ple_kernel(y_ref, p_ref, g_ref, wg_ref, wp_ref, fn_ref, o_ref, *, final_norm):
    y = y_ref[...]
    gate = jax.nn.sigmoid(_dot(_rms(y, g_ref[...]).astype(BF16), wg_ref[...]))
    out = y + gate * _dot(p_ref[0].astype(BF16), wp_ref[...])
    if final_norm:
        out = _rms(out, fn_ref[...])
    o_ref[...] = out


def _ple(y, p_layers, layer, g, w_gate, w_proj, fn, final_norm):
    m, d = y.shape
    dp = p_layers.shape[2]
    tm = _row_tile(m)
    kern = functools.partial(_ple_kernel, final_norm=final_norm)
    return pl.pallas_call(
        kern,
        grid=(m // tm,),
        in_specs=[
            pl.BlockSpec((tm, d), lambda i: (i, 0)),
            pl.BlockSpec((1, tm, dp), lambda i: (layer, i, 0)),
            pl.BlockSpec((1, d), lambda i: (0, 0)),
            pl.BlockSpec((d, d), lambda i: (0, 0)),
            pl.BlockSpec((dp, d), lambda i: (0, 0)),
            pl.BlockSpec((1, d), lambda i: (0, 0)),
        ],
        out_specs=pl.BlockSpec((tm, d), lambda i: (i, 0)),
        out_shape=jax.ShapeDtypeStruct((m, d), F32),
        compiler_params=_cparams("parallel"),
        name="ple",
    )(y, p_layers, g, w_gate, w_proj, fn)


def _diff_proj_kernel(x_ref, g_ref, w_ref, q_ref, kf_ref, vf_ref, *rest, q_scale, with_bf16_kv):
    h = _rms(x_ref[...], g_ref[...]).astype(BF16)
    d = h.shape[1]
    q_ref[...] = (_dot(h, w_ref[:, :d]) * q_scale).astype(BF16)
    k = _dot(h, w_ref[:, d:2 * d])
    kf_ref[...] = k
    v = _dot(h, w_ref[:, 2 * d:])
    vf_ref[...] = v
    if with_bf16_kv:
        kb_ref, vt_ref = rest
        kb_ref[...] = k.astype(BF16)
        vt_ref[0] = v.T.astype(BF16)


def _diff_proj(x, g, w, q_scale, tm, with_bf16_kv):
    m, d = x.shape
    kern = functools.partial(_diff_proj_kernel, q_scale=q_scale, with_bf16_kv=with_bf16_kv)
    out_spec = pl.BlockSpec((tm, d), lambda i: (i, 0))
    out_specs = [out_spec] * 3
    out_shape = [
        jax.ShapeDtypeStruct((m, d), BF16),
        jax.ShapeDtypeStruct((m, d), F32),
        jax.ShapeDtypeStruct((m, d), F32),
    ]
    if with_bf16_kv:
        out_specs += [out_spec, pl.BlockSpec((1, d, tm), lambda i: (i, 0, 0))]
        out_shape += [jax.ShapeDtypeStruct((m, d), BF16), jax.ShapeDtypeStruct((m // tm, d, tm), BF16)]
    return pl.pallas_call(
        kern,
        grid=(m // tm,),
        in_specs=[
            pl.BlockSpec((tm, d), lambda i: (i, 0)),
            pl.BlockSpec((1, d), lambda i: (0, 0)),
            pl.BlockSpec((d, 3 * d), lambda i: (0, 0)),
        ],
        out_specs=out_specs,
        out_shape=out_shape,
        compiler_params=_cparams("parallel"),
        name="diff_proj",
    )(x, g, w)


def _lambda_value(lq1, lk1, lq2, lk2, lam_init):
    s1 = jnp.sum(lq1 * lk1, axis=-1, keepdims=True)
    s2 = jnp.sum(lq2 * lk2, axis=-1, keepdims=True)
    return jnp.exp(s1) - jnp.exp(s2) + lam_init


def _subln(o, g, lam_init):
    return _rms(o, g) * (1.0 - lam_init)


def _diff_attn_kernel(pt_ref, q_ref, k_ref, vt_ref, lq1_ref, lk1_ref, lq2_ref, lk2_ref, sub_ref,
                      subrow_ref, qs_ref, kn_ref, vn_ref, *refs, lam_init, dh, n_pages_step, n_groups,
                      n_decode_steps, n_attn_steps, n_q_blocks):
    k_pages = refs[:n_pages_step]
    v_pages = refs[n_pages_step:2 * n_pages_step]
    (o_ref, os_ref, m_ref, acc_ref, sa_ref, sb_ref, mxa_ref, mxb_ref,
     dm_ref, dl_ref, dacc_ref) = refs[2 * n_pages_step:]
    qi = pl.program_id(1)
    step = pl.program_id(0) * n_q_blocks + qi

    @pl.when(step == 0)
    def _():
        dm_ref[...] = jnp.full_like(dm_ref, NEG_INF)
        dl_ref[...] = jnp.zeros_like(dl_ref)
        dacc_ref[...] = jnp.zeros_like(dacc_ref)

    tk = k_ref.shape[1]
    dv = vt_ref.shape[1]
    q = q_ref[...]
    m_ref[...] = jnp.full_like(m_ref, NEG_INF)
    acc_ref[...] = jnp.zeros_like(acc_ref)
    ones_rows = jnp.ones((acc_ref.shape[1] - dv, tk), BF16)

    def scores(ki, s_ref, mx_ref):
        k = k_ref[ki]
        for i in range(2):
            s = _dot_nt(k[:, i * dh:(i + 1) * dh], q[:, i * dh:(i + 1) * dh])
            s_ref[i] = s
            mx_ref[i] = jnp.max(s, axis=0, keepdims=True)

    def update(ki, s_ref, mx_ref, key_offset=None):
        vt = jnp.concatenate([vt_ref[ki], ones_rows], axis=0)
        for i in range(2):
            s = s_ref[i]
            if key_offset is None:
                mx = mx_ref[i]
            else:
                key = lax.broadcasted_iota(jnp.int32, s.shape, 0) + key_offset
                qry = lax.broadcasted_iota(jnp.int32, s.shape, 1)
                s = jnp.where(key <= qry, s, NEG_INF)
                mx = jnp.max(s, axis=0, keepdims=True)
            m_old = m_ref[i]
            m_new = jnp.maximum(m_old, mx)
            alpha = jnp.exp2(m_old - m_new)
            pr = jnp.exp2(s - m_new)
            acc_ref[i] = alpha * acc_ref[i] + _dot(vt, pr.astype(BF16))
            m_ref[i] = m_new

    scores(0, sa_ref, mxa_ref)

    def pair(j):
        scores(2 * j + 1, sb_ref, mxb_ref)
        update(2 * j, sa_ref, mxa_ref)
        scores(2 * j + 2, sa_ref, mxa_ref)
        update(2 * j + 1, sb_ref, mxb_ref)

    def body(jj, carry):
        pair(2 * jj)
        pair(2 * jj + 1)
        return carry

    lax.fori_loop(0, qi >> 1, body, 0)
    pl.when((qi & 1) == 1)(lambda: pair(qi - 1))
    scores(2 * qi + 1, sb_ref, mxb_ref)
    update(2 * qi, sa_ref, mxa_ref, 0)
    update(2 * qi + 1, sb_ref, mxb_ref, tk)

    lam = _lambda_value(lq1_ref[...], lk1_ref[...], lq2_ref[...], lk2_ref[...], lam_init)
    a1, a2 = acc_ref[0], acc_ref[1]
    r1, r2 = 1.0 / a1[dv:dv + 1], 1.0 / a2[dv:dv + 1]
    o = a1[:dv] * r1 - lam * (a2[:dv] * r2)
    ms = jnp.mean(o * o, axis=0, keepdims=True)
    o = o * lax.rsqrt(ms + EPS) * sub_ref[...] * (1.0 - lam_init)
    o_ref[...] = o.T.astype(BF16)

    def decode():
        group = step % n_groups
        _decode_update(group == 0, group == n_groups - 1, lam, qs_ref, kn_ref, vn_ref, subrow_ref, k_pages, v_pages,
                       os_ref, dm_ref, dl_ref, dacc_ref, lam_init=lam_init, dh=dh)

    if n_decode_steps == n_attn_steps:
        decode()
    else:
        pl.when(step < n_decode_steps)(decode)


def _diff_attn(q, k, vt, batch, seq, t, q_dec, k_new, v_new, cache_k, cache_v, page_table,
               lq1, lk1, lq2, lk2, subln_row, lam_init):
    n_heads = DIFF_HEADS
    assert n_heads & (n_heads - 1) == 0
    m, d = q.shape
    dv = d // n_heads
    dh = dv // 2
    nb = seq // t
    tk = t // 2
    nkb = seq // tk
    assert vt.shape == (m // tk, d, tk)
    n_attn_steps = batch * n_heads * nb
    n_dec, n_pages = page_table.shape
    pps = max(p for p in range(1, PAGES_PER_STEP + 1) if n_pages % p == 0)
    n_groups = n_pages // pps
    n_decode_steps = n_dec * n_groups
    assert n_decode_steps <= n_attn_steps, "decode page groups must fit in the prompt attention grid"
    rows = cache_k.shape[1]
    kern = functools.partial(_diff_attn_kernel, lam_init=lam_init, dh=dh, n_pages_step=pps, n_groups=n_groups,
                             n_decode_steps=n_decode_steps, n_attn_steps=n_attn_steps, n_q_blocks=nb)
    vec = lambda n: pl.BlockSpec((1, n), lambda bh, qi, pt: (0, 0))
    q_map = lambda bh, qi, pt: ((bh // n_heads) * nb + qi, bh % n_heads)
    dstep = lambda bh, qi: jnp.minimum(bh * nb + qi, n_decode_steps - 1)
    tok = pl.BlockSpec((1, n_heads, dv), lambda bh, qi, pt: (dstep(bh, qi) // n_groups, 0, 0))

    def page(u):
        def index(bh, qi, pt):
            s = dstep(bh, qi)
            return (pt[s // n_groups, (s % n_groups) * pps + u], 0, 0)
        return pl.BlockSpec((1, rows, dv), index)

    return pl.pallas_call(
        kern,
        grid_spec=pltpu.PrefetchScalarGridSpec(
            num_scalar_prefetch=1,
            grid=(batch * n_heads, nb),
            in_specs=[
                pl.BlockSpec((t, dv), q_map),
                pl.BlockSpec((nkb, tk, dv), lambda bh, qi, pt: (bh // n_heads, 0, bh % n_heads)),
                pl.BlockSpec((nkb, dv, tk), lambda bh, qi, pt: (bh // n_heads, bh % n_heads, 0)),
                vec(dh), vec(dh), vec(dh), vec(dh),
                pl.BlockSpec((dv, 1), lambda bh, qi, pt: (0, 0)),
                vec(dv), tok, tok, tok,
            ] + [page(u) for u in range(pps)] + [page(u) for u in range(pps)],
            out_specs=[pl.BlockSpec((t, dv), q_map), tok],
            scratch_shapes=[
                pltpu.VMEM((2, 1, t), F32),
                pltpu.VMEM((2, dv + BF16_SUBLANES, t), F32),
                pltpu.VMEM((2, tk, t), F32),
                pltpu.VMEM((2, tk, t), F32),
                pltpu.VMEM((2, 1, t), F32),
                pltpu.VMEM((2, 1, t), F32),
                pltpu.VMEM((2 * n_heads, 1), F32),
                pltpu.VMEM((2 * n_heads, 1), F32),
                pltpu.VMEM((2 * n_heads, dv), F32),
            ],
        ),
        out_shape=[jax.ShapeDtypeStruct((m, d), BF16), jax.ShapeDtypeStruct((n_dec, n_heads, dv), BF16)],
        compiler_params=_cparams("arbitrary", "arbitrary"),
        name="diff_attn",
    )(page_table, q, k.reshape(m // tk, tk, d), vt, lq1, lk1, lq2, lk2, subln_row.reshape(dv, 1),
      subln_row, q_dec, k_new, v_new, *([cache_k] * pps), *([cache_v] * pps))


def _decode_update(first_group, last_group, lam, q_ref, kn_ref, vn_ref, sub_ref, k_refs, v_refs,
                   o_ref, m_ref, l_ref, acc_ref, *, lam_init, dh):
    n_heads = q_ref.shape[1]
    n_pages_step = len(k_refs)
    q = q_ref[0]
    lane = lax.broadcasted_iota(jnp.int32, q.shape, 1)
    first = lane < dh
    qm = jnp.concatenate([jnp.where(first, q, 0.0), jnp.where(first, 0.0, q)], axis=0).astype(BF16)
    s = jnp.concatenate([_dot_nt(qm, kr[0].astype(BF16)) for kr in k_refs], axis=1)
    head_mask = n_heads - 1
    own = ((lax.broadcasted_iota(jnp.int32, s.shape, 1) & head_mask)
           == (lax.broadcasted_iota(jnp.int32, s.shape, 0) & head_mask))
    s = jnp.where(own, s, NEG_INF)
    m_old = jnp.where(first_group, NEG_INF, m_ref[...])
    l_old = jnp.where(first_group, 0.0, l_ref[...])
    acc_old = jnp.where(first_group, 0.0, acc_ref[...])
    m_new = jnp.maximum(m_old, jnp.max(s, axis=-1, keepdims=True))
    alpha = jnp.exp2(m_old - m_new)
    pr = jnp.exp2(s - m_new)
    l_new = alpha * l_old + jnp.sum(pr, axis=-1, keepdims=True)
    pb = pr.astype(BF16)
    rows = k_refs[0].shape[1]
    pv = _dot(pb[:, :rows], v_refs[0][0].astype(BF16))
    for u in range(1, n_pages_step):
        pv += _dot(pb[:, u * rows:(u + 1) * rows], v_refs[u][0].astype(BF16))
    acc_new = alpha * acc_old + pv
    m_ref[...] = m_new
    l_ref[...] = l_new
    acc_ref[...] = acc_new

    @pl.when(last_group)
    def _():
        kn, vn = kn_ref[0], vn_ref[0]
        prod = q * kn
        sn = jnp.concatenate([jnp.sum(jnp.where(first, prod, 0.0), axis=-1, keepdims=True),
                              jnp.sum(jnp.where(first, 0.0, prod), axis=-1, keepdims=True)], axis=0)
        m_fin = jnp.maximum(m_new, sn)
        a2 = jnp.exp2(m_new - m_fin)
        pn = jnp.exp2(sn - m_fin)
        l_fin = a2 * l_new + pn
        acc_fin = a2 * acc_new + pn * jnp.concatenate([vn, vn], axis=0)
        o2 = acc_fin / l_fin
        o = o2[:n_heads] - lam * o2[n_heads:]
        o_ref[0] = _subln(o, sub_ref[...], lam_init).astype(BF16)


def _rotary_tables(start, length, dk, rows=None):
    angle = 1.0 / (ROPE_BASE ** jnp.linspace(0.0, 1.0, dk // 2, dtype=F32))
    angle = jnp.repeat(angle, 2)
    pos = start + jnp.arange(length, dtype=F32)
    th = pos[:, None] * angle[None, :]
    sign = jnp.where(jnp.arange(dk) % 2 == 0, -1.0, 1.0).astype(F32)
    cos_t = jnp.cos(th)
    sin_t = jnp.sin(th) * sign[None, :]
    if rows is not None:
        cos_t = jnp.broadcast_to(cos_t, (rows, cos_t.shape[1]))
        sin_t = jnp.broadcast_to(sin_t, (rows, sin_t.shape[1]))
    return cos_t, sin_t


def _rotary_half_tables(length, dk, block=128):
    angle = 1.0 / (ROPE_BASE ** jnp.linspace(0.0, 1.0, dk // 2, dtype=F32))
    n_hi = -(-length // block)
    th_hi = (jnp.arange(n_hi, dtype=F32) * block)[:, None] * angle[None, :]
    th_lo = jnp.arange(block, dtype=F32)[:, None] * angle[None, :]
    ch, sh = jnp.cos(th_hi)[:, None, :], jnp.sin(th_hi)[:, None, :]
    cl, sl = jnp.cos(th_lo)[None, :, :], jnp.sin(th_lo)[None, :, :]
    cos_t = (ch * cl - sh * sl).reshape(n_hi * block, dk // 2)[:length]
    sin_t = (sh * cl + ch * sl).reshape(n_hi * block, dk // 2)[:length]
    return cos_t, sin_t


def _split_cols_kernel(w_ref, o_ref):
    dk = w_ref.shape[1]
    half = dk // 2
    src = lax.broadcasted_iota(jnp.int32, (dk, dk), 0)
    dst = lax.broadcasted_iota(jnp.int32, (dk, dk), 1)
    perm = (src == jnp.where(dst < half, 2 * dst, 2 * (dst - half) + 1)).astype(BF16)
    o_ref[...] = _dot(w_ref[...], perm).astype(BF16)


def _split_even_odd_columns(w, n_cols, dk):
    rows = w.shape[0]
    return pl.pallas_call(
        _split_cols_kernel,
        grid=(n_cols // dk,),
        in_specs=[pl.BlockSpec((rows, dk), lambda c: (0, c))],
        out_specs=pl.BlockSpec((rows, dk), lambda c: (0, c)),
        out_shape=jax.ShapeDtypeStruct((rows, n_cols), BF16),
        compiler_params=_cparams("parallel"),
        name="split_cols",
    )(w)


def _merge_even_odd_rows(s, dk):
    lead = s.shape[:-2]
    return s.reshape(*lead, 2, dk // 2, s.shape[-1]).swapaxes(-3, -2).reshape(*lead, dk, s.shape[-1])


def _decay_tables(n_heads, c):
    lg = jnp.log1p(-jnp.exp2(-5.0 - jnp.arange(n_heads, dtype=F32)))
    idx = jnp.arange(c, dtype=F32)
    rel = idx[:, None] - idx[None, :]
    dec_in = jnp.where(rel[None] >= 0, jnp.exp(lg[:, None, None] * jnp.maximum(rel, 0.0)[None]), 0.0)
    dec_q = jnp.exp(lg[:, None] * (idx + 1.0)[None])[:, :, None]
    dec_k = jnp.exp(lg[:, None] * (c - 1.0 - idx)[None])[:, :, None]
    return dec_in, dec_q, dec_k


def kernel(x_prompt, x_sample, state_ret, cache_k, cache_v, page_table, p_prompt, p_sample, norm_mix, ret_w_in, ret_w_out, diff_w_in, diff_w_out, diff_lambda_q1, diff_lambda_k1, diff_lambda_q2, diff_lambda_k2, diff_subln, norm_ffn, w_up, w_down, ple_norm, w_ple_gate, w_ple_proj, final_norm):
    batch, seq, d = x_prompt.shape
    nb, dec_seq, _ = x_sample.shape
    assert dec_seq == 1
    depth = norm_mix.shape[0]
    mp, ms = batch * seq, nb * dec_seq
    yp = x_prompt.reshape(mp, d)
    ys = x_sample.reshape(ms, d)
    row = lambda a: a.reshape(1, -1)
    fn = row(final_norm)

    ret_p, ret_s, kp_rows, vp_rows, ks_rows, vs_rows = [], [], [], [], [], []
    for i in range(depth):
        g_mix = row(norm_mix[i])
        if i % N_MIXERS == 0:
            r = i // N_MIXERS
            n_heads = RET_HEADS
            dk = d // n_heads
            dv = 2 * dk
            w_in = ret_w_in[r].astype(BF16)
            w_out = ret_w_out[r].astype(BF16)
            k_scale = dk ** -0.5
            w_qk_split = _split_even_odd_columns(w_in, 2 * d, dk)
            cos_p, sin_p = _rotary_half_tables(seq, dk)
            proj_p = _ret_proj(yp, g_mix, w_qk_split, w_in, cos_p, sin_p, k_scale, dk, True)
            dec_in, dec_q, dec_k = _decay_tables(n_heads, RET_CHUNK)
            og_p, s_p = _ret_scan(proj_p, batch, seq, dec_in, dec_q, dec_k)
            yp = _matmul_res(og_p, w_out, yp)
            ret_p.append(_merge_even_odd_rows(s_p, dk).astype(x_prompt.dtype))
            cos_s, sin_s = _rotary_tables(PAST_LEN, 1, dk, rows=ms)
            proj_s = _ret_proj(ys, g_mix, w_in, w_in, cos_s, sin_s, k_scale, dk, False).astype(F32)
            gammas = tuple(math.exp(math.log1p(-2.0 ** (-5.0 - h))) for h in range(n_heads))
            q_col = proj_s[:, :d].reshape(ms, n_heads, 1, dk)
            k_col = proj_s[:, d:2 * d].reshape(ms, n_heads, 1, dk)
            v_row = proj_s[:, 2 * d:2 * d + n_heads * dv].reshape(ms, n_heads, 1, dv)
            g_row = proj_s[:, 2 * d + n_heads * dv:].reshape(ms, n_heads, 1, dv)
            og_s, s_s = _ret_step(q_col, k_col, v_row, g_row, state_ret[r].astype(F32), gammas)
            ys = _matmul_res(og_s.reshape(ms, n_heads * dv), w_out, ys)
            ret_s.append(s_s.astype(state_ret.dtype))
        else:
            di = i // N_MIXERS
            n_heads = DIFF_HEADS
            dv = d // n_heads
            dh = dv // 2
            lam_init = 0.8 - 0.6 * math.exp(-0.3 * i)
            w_in = diff_w_in[di].astype(BF16)
            w_out = diff_w_out[di].astype(BF16)
            lams = [row(a[di].astype(F32)) for a in (diff_lambda_q1, diff_lambda_k1, diff_lambda_q2, diff_lambda_k2)]
            sub = row(diff_subln[di].astype(F32))
            q_scale = dh ** -0.5 * math.log2(math.e)
            t = ATTN_BLOCK if seq % ATTN_BLOCK == 0 else seq
            q_p, kf_p, vf_p, kb_p, vt_p = _diff_proj(yp, g_mix, w_in, q_scale, t // 2, True)
            q_s, kf_s, vf_s = _diff_proj(ys, g_mix, w_in, q_scale, ms, False)
            n_phys, page_size = cache_k.shape[1], cache_k.shape[2]
            ck = cache_k[di].reshape(n_phys, page_size * n_heads, dv)
            cv = cache_v[di].reshape(n_phys, page_size * n_heads, dv)
            o_p, o_s = _diff_attn(q_p, kb_p, vt_p, batch, seq, t,
                                  q_s.astype(F32).reshape(ms, n_heads, dv), kf_s.reshape(ms, n_heads, dv),
                                  vf_s.reshape(ms, n_heads, dv), ck, cv, page_table, *lams, sub, lam_init)
            yp = _matmul_res(o_p, w_out, yp)
            kp_rows.append(kf_p.reshape(batch, seq, n_heads, dv))
            vp_rows.append(vf_p.reshape(batch, seq, n_heads, dv))
            ys = _matmul_res(o_s.reshape(ms, d), w_out, ys)
            ks_rows.append(kf_s.reshape(nb, dec_seq, n_heads, dv))
            vs_rows.append(vf_s.reshape(nb, dec_seq, n_heads, dv))
        g_ffn = row(norm_ffn[i])
        wu, wd = w_up[i].astype(BF16), w_down[i].astype(BF16)
        yp = _ffn(yp, g_ffn, wu, wd)
        ys = _ffn(ys, g_ffn, wu, wd)
        g_ple = row(ple_norm[i])
        wg, wp = w_ple_gate[i].astype(BF16), w_ple_proj[i].astype(BF16)
        last = i == depth - 1
        yp = _ple(yp, p_prompt.reshape(depth, mp, -1), i, g_ple, wg, wp, fn, last)
        ys = _ple(ys, p_sample.reshape(depth, ms, -1), i, g_ple, wg, wp, fn, last)

    y_prompt = yp.reshape(batch, seq, d)
    y_sample = ys.reshape(nb, dec_seq, d)
    return (y_prompt, y_sample, jnp.stack(ret_p), jnp.stack(ret_s),
            jnp.stack(kp_rows), jnp.stack(vp_rows), jnp.stack(ks_rows), jnp.stack(vs_rows))
```

```python
import functools
import math

import jax
import jax.numpy as jnp
from jax import lax
from jax.experimental import pallas as pl
from jax.experimental.pallas import tpu as pltpu

EPS = 1e-6
NEG_INF = -1e30
N_MIXERS = 2
RET_HEADS = 4
RET_CHUNK = 256
ROPE_BASE = 10000.0
DIFF_HEADS = 8
PAST_LEN = 8192

F32 = jnp.float32
BF16 = jnp.bfloat16
BF16_SUBLANES = 16
VMEM_LIMIT_BYTES = 48 * 1024 * 1024
ROW_TILE = 1024
ATTN_BLOCK = 512
PAGES_PER_STEP = 8
RET_CHUNKS_PER_STEP = 8
RET_PROJ_BLOCK = 1024
RET_STEP_ROWS = 2
COLUMN_TILE = 128
FFN_BLOCK = 1024


def _cparams(*sem):
    return pltpu.CompilerParams(dimension_semantics=sem, vmem_limit_bytes=VMEM_LIMIT_BYTES)


def _row_tile(m, pref=None):
    pref = ROW_TILE if pref is None else pref
    return pref if m % pref == 0 else m


def _rms(x, g):
    return x * lax.rsqrt(jnp.mean(x * x, axis=-1, keepdims=True) + EPS) * g


def _dot(a, b):
    return jnp.dot(a, b, preferred_element_type=F32)


def _dot_nt(a, b):
    return lax.dot_general(a, b, (((1,), (1,)), ((), ())), preferred_element_type=F32)


def _dot_tn(a, b):
    return lax.dot_general(a, b, (((0,), (0,)), ((), ())), preferred_element_type=F32)


def _ret_proj_kernel(x_ref, g_ref, wqk_ref, w_ref, cos_ref, sin_ref, o_ref, h_ref, *, n_qk_blocks, k_scale, dk,
                     split_halves):
    j = pl.program_id(1)

    @pl.when(j == 0)
    def _():
        h_ref[...] = _rms(x_ref[...], g_ref[...]).astype(BF16)

    @pl.when(j < n_qk_blocks)
    def _():
        acc = _dot(h_ref[...], wqk_ref[...])
        cos_t, sin_t = cos_ref[...], sin_ref[...]
        scale = jnp.where(j >= n_qk_blocks // 2, k_scale, 1.0).astype(F32)
        for hh in range(acc.shape[1] // dk):
            a = acc[:, hh * dk:(hh + 1) * dk]
            if split_halves:
                half = dk // 2
                ev, od = a[:, :half], a[:, half:]
                o_ref[:, hh * dk:hh * dk + half] = ((ev * cos_t - od * sin_t) * scale).astype(BF16)
                o_ref[:, hh * dk + half:(hh + 1) * dk] = ((od * cos_t + ev * sin_t) * scale).astype(BF16)
            else:
                even = (lax.broadcasted_iota(jnp.int32, a.shape, 1) & 1) == 0
                swapped = jnp.where(even, pltpu.roll(a, dk - 1, 1), pltpu.roll(a, 1, 1))
                o_ref[:, hh * dk:(hh + 1) * dk] = ((a * cos_t + swapped * sin_t) * scale).astype(BF16)

    @pl.when(j >= n_qk_blocks)
    def _():
        o_ref[...] = _dot(h_ref[...], w_ref[...]).astype(BF16)


def _ret_proj(x, g, w_qk, w, cos_t, sin_t, k_scale, dk, split_halves):
    m, d = x.shape
    n = w.shape[1]
    rows_per_seq, tw = cos_t.shape
    assert tw == (dk // 2 if split_halves else dk)
    tm = _row_tile(rows_per_seq)
    n_pos_blocks = rows_per_seq // tm
    tn = RET_PROJ_BLOCK if d % RET_PROJ_BLOCK == 0 else dk
    n_qk_blocks = 2 * d // tn
    kern = functools.partial(_ret_proj_kernel, n_qk_blocks=n_qk_blocks, k_scale=k_scale, dk=dk,
                             split_halves=split_halves)
    return pl.pallas_call(
        kern,
        grid=(m // tm, n // tn),
        in_specs=[
            pl.BlockSpec((tm, d), lambda i, j: (i, 0)),
            pl.BlockSpec((1, d), lambda i, j: (0, 0)),
            pl.BlockSpec((d, tn), lambda i, j: (0, jnp.minimum(j, n_qk_blocks - 1))),
            pl.BlockSpec((d, tn), lambda i, j: (0, jnp.maximum(j, n_qk_blocks))),
            pl.BlockSpec((tm, tw), lambda i, j: (i % n_pos_blocks, 0)),
            pl.BlockSpec((tm, tw), lambda i, j: (i % n_pos_blocks, 0)),
        ],
        out_specs=pl.BlockSpec((tm, tn), lambda i, j: (i, j)),
        out_shape=jax.ShapeDtypeStruct((m, n), BF16),
        scratch_shapes=[pltpu.VMEM((tm, d), BF16)],
        compiler_params=_cparams("parallel", "arbitrary"),
        name="ret_proj",
    )(x, g, w_qk, w, cos_t, sin_t)


def _ret_scan_kernel(q_ref, k_ref, v_ref, g_ref, din_ref, dq_ref, dk_ref, o_ref, sfin_ref, s_ref):
    c = pl.program_id(1)

    @pl.when(c == 0)
    def _():
        s_ref[...] = jnp.zeros_like(s_ref)

    chunk = din_ref.shape[1]
    dec_chunk = dq_ref[0, chunk - 1:chunk, :]
    for u in range(q_ref.shape[0] // chunk):
        rows = slice(u * chunk, (u + 1) * chunk)
        q, k, v = q_ref[rows, :], k_ref[rows, :], v_ref[rows, :]
        s_old = s_ref[...]
        scores = _dot_nt(q, k) * din_ref[0]
        o = _dot(scores.astype(BF16), v) + _dot(q, s_old.astype(BF16)) * dq_ref[0]
        k_dec = (k.astype(F32) * dk_ref[0]).astype(BF16)
        s_ref[...] = s_old * dec_chunk + _dot_tn(k_dec, v)
        mu = jnp.mean(o, axis=-1, keepdims=True)
        d = o - mu
        on = d * lax.rsqrt(jnp.mean(d * d, axis=-1, keepdims=True) + EPS)
        gate = g_ref[rows, :].astype(F32)
        o_ref[rows, :] = (gate * jax.nn.sigmoid(gate) * on).astype(BF16)

    @pl.when(c == pl.num_programs(1) - 1)
    def _():
        sfin_ref[0, 0] = s_ref[...]


def _ret_scan(proj, batch, seq, dec_in, dec_q, dec_k):
    n_heads = RET_HEADS
    m = proj.shape[0]
    dk = proj.shape[1] // (6 * n_heads)
    dv = 2 * dk
    assert seq % RET_CHUNK == 0
    chunks_per_step = RET_CHUNKS_PER_STEP if (seq // RET_CHUNK) % RET_CHUNKS_PER_STEP == 0 else 1
    c = RET_CHUNK * chunks_per_step
    nc = seq // c
    kern = _ret_scan_kernel
    row = lambda bh, ci: (bh // n_heads) * nc + ci
    return pl.pallas_call(
        kern,
        grid=(batch * n_heads, nc),
        in_specs=[
            pl.BlockSpec((c, dk), lambda bh, ci: (row(bh, ci), bh % n_heads)),
            pl.BlockSpec((c, dk), lambda bh, ci: (row(bh, ci), n_heads + bh % n_heads)),
            pl.BlockSpec((c, dv), lambda bh, ci: (row(bh, ci), n_heads + bh % n_heads)),
            pl.BlockSpec((c, dv), lambda bh, ci: (row(bh, ci), 2 * n_heads + bh % n_heads)),
            pl.BlockSpec((1, RET_CHUNK, RET_CHUNK), lambda bh, ci: (bh % n_heads, 0, 0)),
            pl.BlockSpec((1, RET_CHUNK, 1), lambda bh, ci: (bh % n_heads, 0, 0)),
            pl.BlockSpec((1, RET_CHUNK, 1), lambda bh, ci: (bh % n_heads, 0, 0)),
        ],
        out_specs=[
            pl.BlockSpec((c, dv), lambda bh, ci: (row(bh, ci), bh % n_heads)),
            pl.BlockSpec((1, 1, dk, dv), lambda bh, ci: (bh // n_heads, bh % n_heads, 0, 0)),
        ],
        out_shape=[
            jax.ShapeDtypeStruct((m, n_heads * dv), BF16),
            jax.ShapeDtypeStruct((batch, n_heads, dk, dv), F32),
        ],
        scratch_shapes=[pltpu.VMEM((dk, dv), F32)],
        compiler_params=_cparams("parallel", "arbitrary"),
        name="ret_scan",
    )(proj, proj, proj, proj, dec_in, dec_q, dec_k)


def _ret_step_kernel(q_ref, k_ref, v_ref, g_ref, s_ref, o_ref, snew_ref, *, gammas):
    dk, dv = s_ref.shape[2], s_ref.shape[3]

    def column(row):
        tile = jnp.broadcast_to(row, (COLUMN_TILE, dk)).T
        return jnp.concatenate([tile] * (dv // COLUMN_TILE), axis=1)

    for r in range(s_ref.shape[0]):
        for h, gamma in enumerate(gammas):
            q_row, k_row = q_ref[r, h], k_ref[r, h]
            q, k = column(q_row), column(k_row)
            v = v_ref[r, h]
            s_old = s_ref[r, h]
            qk = jnp.sum(q_row * k_row, axis=-1, keepdims=True)
            o = qk * v + gamma * jnp.sum(q * s_old, axis=0, keepdims=True)
            snew_ref[r, h] = s_old * gamma + k * v
            mu = jnp.mean(o, axis=-1, keepdims=True)
            d = o - mu
            on = d * lax.rsqrt(jnp.mean(d * d, axis=-1, keepdims=True) + EPS)
            gate = g_ref[r, h]
            o_ref[r, h] = (gate * jax.nn.sigmoid(gate) * on).astype(BF16)


def _ret_step(q_col, k_col, v_row, g_row, state, gammas):
    nb, n_heads, dk, dv = state.shape
    rb = RET_STEP_ROWS if nb % RET_STEP_ROWS == 0 else 1
    idx = lambda b: (b, 0, 0, 0)
    kern = functools.partial(_ret_step_kernel, gammas=gammas)
    return pl.pallas_call(
        kern,
        grid=(nb // rb,),
        in_specs=[
            pl.BlockSpec((rb, n_heads, 1, dk), idx),
            pl.BlockSpec((rb, n_heads, 1, dk), idx),
            pl.BlockSpec((rb, n_heads, 1, dv), idx),
            pl.BlockSpec((rb, n_heads, 1, dv), idx),
            pl.BlockSpec((rb, n_heads, dk, dv), idx),
        ],
        out_specs=[
            pl.BlockSpec((rb, n_heads, 1, dv), idx),
            pl.BlockSpec((rb, n_heads, dk, dv), idx),
        ],
        out_shape=[
            jax.ShapeDtypeStruct((nb, n_heads, 1, dv), BF16),
            jax.ShapeDtypeStruct(state.shape, F32),
        ],
        compiler_params=_cparams("parallel"),
        name="ret_step",
    )(q_col, k_col, v_row, g_row, state)


def _matmul_res_kernel(a_ref, w_ref, y_ref, o_ref):
    o_ref[...] = y_ref[...] + _dot(a_ref[...], w_ref[...])


def _matmul_res(a, w, y):
    m, k = a.shape
    n = w.shape[1]
    tm = _row_tile(m)
    return pl.pallas_call(
        _matmul_res_kernel,
        grid=(m // tm,),
        in_specs=[
            pl.BlockSpec((tm, k), lambda i: (i, 0)),
            pl.BlockSpec((k, n), lambda i: (0, 0)),
            pl.BlockSpec((tm, n), lambda i: (i, 0)),
        ],
        out_specs=pl.BlockSpec((tm, n), lambda i: (i, 0)),
        out_shape=jax.ShapeDtypeStruct((m, n), F32),
        compiler_params=_cparams("parallel"),
        name="mix_out",
    )(a, w, y)


def _ffn_kernel(y_ref, g_ref, wu_ref, wd_ref, o_ref, h_ref, acc_ref, *, n_blocks):
    j = pl.program_id(1)
    last = n_blocks - 1

    @pl.when(j == 0)
    def _():
        h_ref[...] = _rms(y_ref[...], g_ref[...]).astype(BF16)

    def block():
        u = jnp.maximum(_dot(h_ref[...], wu_ref[...]), 0.0)
        return _dot((u * u).astype(BF16), wd_ref[...])

    if n_blocks == 1:
        o_ref[...] = y_ref[...] + block()
    else:
        @pl.when(j == 0)
        def _():
            acc_ref[...] = block()

        @pl.when((j > 0) & (j < last))
        def _():
            acc_ref[...] += block()

        @pl.when(j == last)
        def _():
            o_ref[...] = y_ref[...] + acc_ref[...] + block()


def _ffn(y, g, w_up, w_down):
    m, d = y.shape
    f = w_up.shape[1]
    tm, tf = _row_tile(m), FFN_BLOCK
    return pl.pallas_call(
        functools.partial(_ffn_kernel, n_blocks=f // tf),
        grid=(m // tm, f // tf),
        in_specs=[
            pl.BlockSpec((tm, d), lambda i, j: (i, 0)),
            pl.BlockSpec((1, d), lambda i, j: (0, 0)),
            pl.BlockSpec((d, tf), lambda i, j: (0, j)),
            pl.BlockSpec((tf, d), lambda i, j: (j, 0)),
        ],
        out_specs=pl.BlockSpec((tm, d), lambda i, j: (i, 0)),
        out_shape=jax.ShapeDtypeStruct((m, d), F32),
        scratch_shapes=[pltpu.VMEM((tm, d), BF16), pltpu.VMEM((tm, d), F32)],
        compiler_params=_cparams("parallel", "arbitrary"),
        name="ffn",
    )(y, g, w_up, w_down)


def _ple_kernel(y_ref, p_ref, g_ref, wg_ref, wp_ref, fn_ref, o_ref, *, final_norm):
    y = y_ref[...]
    gate = jax.nn.sigmoid(_dot(_rms(y, g_ref[...]).astype(BF16), wg_ref[...]))
    out = y + gate * _dot(p_ref[0].astype(BF16), wp_ref[...])
    if final_norm:
        out = _rms(out, fn_ref[...])
    o_ref[...] = out


def _ple(y, p_layers, layer, g, w_gate, w_proj, fn, final_norm):
    m, d = y.shape
    dp = p_layers.shape[2]
    tm = _row_tile(m)
    kern = functools.partial(_ple_kernel, final_norm=final_norm)
    return pl.pallas_call(
        kern,
        grid=(m // tm,),
        in_specs=[
            pl.BlockSpec((tm, d), lambda i: (i, 0)),
            pl.BlockSpec((1, tm, dp), lambda i: (layer, i, 0)),
            pl.BlockSpec((1, d), lambda i: (0, 0)),
            pl.BlockSpec((d, d), lambda i: (0, 0)),
            pl.BlockSpec((dp, d), lambda i: (0, 0)),
            pl.BlockSpec((1, d), lambda i: (0, 0)),
        ],
        out_specs=pl.BlockSpec((tm, d), lambda i: (i, 0)),
        out_shape=jax.ShapeDtypeStruct((m, d), F32),
        compiler_params=_cparams("parallel"),
        name="ple",
    )(y, p_layers, g, w_gate, w_proj, fn)


def _diff_proj_kernel(x_ref, g_ref, w_ref, q_ref, kf_ref, vf_ref, *rest, q_scale, with_bf16_kv):
    h = _rms(x_ref[...], g_ref[...]).astype(BF16)
    d = h.shape[1]
    q_ref[...] = (_dot(h, w_ref[:, :d]) * q_scale).astype(BF16)
    k = _dot(h, w_ref[:, d:2 * d])
    kf_ref[...] = k
    v = _dot(h, w_ref[:, 2 * d:])
    vf_ref[...] = v
    if with_bf16_kv:
        kb_ref, vt_ref = rest
        kb_ref[...] = k.astype(BF16)
        tk = vt_ref.shape[2]
        for u in range(vt_ref.shape[0]):
            vt_ref[u] = v[u * tk:(u + 1) * tk, :].T.astype(BF16)


def _diff_proj(x, g, w, q_scale, tm, vt_cols=None):
    m, d = x.shape
    with_bf16_kv = vt_cols is not None
    kern = functools.partial(_diff_proj_kernel, q_scale=q_scale, with_bf16_kv=with_bf16_kv)
    out_spec = pl.BlockSpec((tm, d), lambda i: (i, 0))
    out_specs = [out_spec] * 3
    out_shape = [
        jax.ShapeDtypeStruct((m, d), BF16),
        jax.ShapeDtypeStruct((m, d), F32),
        jax.ShapeDtypeStruct((m, d), F32),
    ]
    if with_bf16_kv:
        assert tm % vt_cols == 0
        out_specs += [out_spec, pl.BlockSpec((tm // vt_cols, d, vt_cols), lambda i: (i, 0, 0))]
        out_shape += [jax.ShapeDtypeStruct((m, d), BF16), jax.ShapeDtypeStruct((m // vt_cols, d, vt_cols), BF16)]
    return pl.pallas_call(
        kern,
        grid=(m // tm,),
        in_specs=[
            pl.BlockSpec((tm, d), lambda i: (i, 0)),
            pl.BlockSpec((1, d), lambda i: (0, 0)),
            pl.BlockSpec((d, 3 * d), lambda i: (0, 0)),
        ],
        out_specs=out_specs,
        out_shape=out_shape,
        compiler_params=_cparams("parallel"),
        name="diff_proj",
    )(x, g, w)


def _lambda_value(lq1, lk1, lq2, lk2, lam_init):
    s1 = jnp.sum(lq1 * lk1, axis=-1, keepdims=True)
    s2 = jnp.sum(lq2 * lk2, axis=-1, keepdims=True)
    return jnp.exp(s1) - jnp.exp(s2) + lam_init


def _subln(o, g, lam_init):
    return _rms(o, g) * (1.0 - lam_init)


def _diff_attn_kernel(pt_ref, q_ref, k_ref, vt_ref, lq1_ref, lk1_ref, lq2_ref, lk2_ref, sub_ref,
                      subrow_ref, qs_ref, kn_ref, vn_ref, *refs, lam_init, dh, n_pages_step, n_groups,
                      n_decode_steps, n_attn_steps, n_q_blocks):
    k_pages = refs[:n_pages_step]
    v_pages = refs[n_pages_step:2 * n_pages_step]
    (o_ref, os_ref, m_ref, acc_ref, sa_ref, sb_ref, mxa_ref, mxb_ref,
     dm_ref, dl_ref, dacc_ref) = refs[2 * n_pages_step:]
    qi = pl.program_id(1)
    step = pl.program_id(0) * n_q_blocks + qi

    @pl.when(step == 0)
    def _():
        dm_ref[...] = jnp.full_like(dm_ref, NEG_INF)
        dl_ref[...] = jnp.zeros_like(dl_ref)
        dacc_ref[...] = jnp.zeros_like(dacc_ref)

    tk = k_ref.shape[1]
    dv = vt_ref.shape[1]
    q = q_ref[...]
    m_ref[...] = jnp.full_like(m_ref, NEG_INF)
    acc_ref[...] = jnp.zeros_like(acc_ref)
    ones_rows = jnp.ones((acc_ref.shape[1] - dv, tk), BF16)

    def scores(ki, s_ref, mx_ref):
        k = k_ref[ki]
        for i in range(2):
            s = _dot_nt(k[:, i * dh:(i + 1) * dh], q[:, i * dh:(i + 1) * dh])
            s_ref[i] = s
            mx_ref[i] = jnp.max(s, axis=0, keepdims=True)

    def update(ki, s_ref, mx_ref, key_offset=None):
        vt = jnp.concatenate([vt_ref[ki], ones_rows], axis=0)
        for i in range(2):
            s = s_ref[i]
            if key_offset is None:
                mx = mx_ref[i]
            else:
                key = lax.broadcasted_iota(jnp.int32, s.shape, 0) + key_offset
                qry = lax.broadcasted_iota(jnp.int32, s.shape, 1)
                s = jnp.where(key <= qry, s, NEG_INF)
                mx = jnp.max(s, axis=0, keepdims=True)
            m_old = m_ref[i]
            m_new = jnp.maximum(m_old, mx)
            alpha = jnp.exp2(m_old - m_new)
            pr = jnp.exp2(s - m_new)
            acc_ref[i] = alpha * acc_ref[i] + _dot(vt, pr.astype(BF16))
            m_ref[i] = m_new

    scores(0, sa_ref, mxa_ref)

    def pair(j):
        scores(2 * j + 1, sb_ref, mxb_ref)
        update(2 * j, sa_ref, mxa_ref)
        scores(2 * j + 2, sa_ref, mxa_ref)
        update(2 * j + 1, sb_ref, mxb_ref)

    def body(jj, carry):
        pair(2 * jj)
        pair(2 * jj + 1)
        return carry

    lax.fori_loop(0, qi >> 1, body, 0)
    pl.when((qi & 1) == 1)(lambda: pair(qi - 1))
    scores(2 * qi + 1, sb_ref, mxb_ref)
    update(2 * qi, sa_ref, mxa_ref, 0)
    update(2 * qi + 1, sb_ref, mxb_ref, tk)

    lam = _lambda_value(lq1_ref[...], lk1_ref[...], lq2_ref[...], lk2_ref[...], lam_init)
    a1, a2 = acc_ref[0], acc_ref[1]
    r1, r2 = 1.0 / a1[dv:dv + 1], 1.0 / a2[dv:dv + 1]
    o = a1[:dv] * r1 - lam * (a2[:dv] * r2)
    ms = jnp.mean(o * o, axis=0, keepdims=True)
    o = o * lax.rsqrt(ms + EPS) * sub_ref[...] * (1.0 - lam_init)
    o_ref[...] = o.T.astype(BF16)

    def decode():
        group = step % n_groups
        _decode_update(group == 0, group == n_groups - 1, lam, qs_ref, kn_ref, vn_ref, subrow_ref, k_pages, v_pages,
                       os_ref, dm_ref, dl_ref, dacc_ref, lam_init=lam_init, dh=dh)

    if n_decode_steps == n_attn_steps:
        decode()
    else:
        pl.when(step < n_decode_steps)(decode)


def _diff_attn(q, k, vt, batch, seq, t, q_dec, k_new, v_new, cache_k, cache_v, page_table,
               lq1, lk1, lq2, lk2, subln_row, lam_init):
    n_heads = DIFF_HEADS
    assert n_heads & (n_heads - 1) == 0
    m, d = q.shape
    dv = d // n_heads
    dh = dv // 2
    nb = seq // t
    tk = t // 2
    nkb = seq // tk
    assert vt.shape == (m // tk, d, tk)
    n_attn_steps = batch * n_heads * nb
    n_dec, n_pages = page_table.shape
    pps = max(p for p in range(1, PAGES_PER_STEP + 1) if n_pages % p == 0)
    n_groups = n_pages // pps
    n_decode_steps = n_dec * n_groups
    assert n_decode_steps <= n_attn_steps, "decode page groups must fit in the prompt attention grid"
    rows = cache_k.shape[1]
    kern = functools.partial(_diff_attn_kernel, lam_init=lam_init, dh=dh, n_pages_step=pps, n_groups=n_groups,
                             n_decode_steps=n_decode_steps, n_attn_steps=n_attn_steps, n_q_blocks=nb)
    vec = lambda n: pl.BlockSpec((1, n), lambda bh, qi, pt: (0, 0))
    q_map = lambda bh, qi, pt: ((bh // n_heads) * nb + qi, bh % n_heads)
    dstep = lambda bh, qi: jnp.minimum(bh * nb + qi, n_decode_steps - 1)
    tok = pl.BlockSpec((1, n_heads, dv), lambda bh, qi, pt: (dstep(bh, qi) // n_groups, 0, 0))

    def page(u):
        def index(bh, qi, pt):
            s = dstep(bh, qi)
            return (pt[s // n_groups, (s % n_groups) * pps + u], 0, 0)
        return pl.BlockSpec((1, rows, dv), index)

    return pl.pallas_call(
        kern,
        grid_spec=pltpu.PrefetchScalarGridSpec(
            num_scalar_prefetch=1,
            grid=(batch * n_heads, nb),
            in_specs=[
                pl.BlockSpec((t, dv), q_map),
                pl.BlockSpec((nkb, tk, dv), lambda bh, qi, pt: (bh // n_heads, 0, bh % n_heads)),
                pl.BlockSpec((nkb, dv, tk), lambda bh, qi, pt: (bh // n_heads, bh % n_heads, 0)),
                vec(dh), vec(dh), vec(dh), vec(dh),
                pl.BlockSpec((dv, 1), lambda bh, qi, pt: (0, 0)),
                vec(dv), tok, tok, tok,
            ] + [page(u) for u in range(pps)] + [page(u) for u in range(pps)],
            out_specs=[pl.BlockSpec((t, dv), q_map), tok],
            scratch_shapes=[
                pltpu.VMEM((2, 1, t), F32),
                pltpu.VMEM((2, dv + BF16_SUBLANES, t), F32),
                pltpu.VMEM((2, tk, t), F32),
                pltpu.VMEM((2, tk, t), F32),
                pltpu.VMEM((2, 1, t), F32),
                pltpu.VMEM((2, 1, t), F32),
                pltpu.VMEM((2 * n_heads, 1), F32),
                pltpu.VMEM((2 * n_heads, 1), F32),
                pltpu.VMEM((2 * n_heads, dv), F32),
            ],
        ),
        out_shape=[jax.ShapeDtypeStruct((m, d), BF16), jax.ShapeDtypeStruct((n_dec, n_heads, dv), BF16)],
        compiler_params=_cparams("arbitrary", "arbitrary"),
        name="diff_attn",
    )(page_table, q, k.reshape(m // tk, tk, d), vt, lq1, lk1, lq2, lk2, subln_row.reshape(dv, 1),
      subln_row, q_dec, k_new, v_new, *([cache_k] * pps), *([cache_v] * pps))


def _decode_update(first_group, last_group, lam, q_ref, kn_ref, vn_ref, sub_ref, k_refs, v_refs,
                   o_ref, m_ref, l_ref, acc_ref, *, lam_init, dh):
    n_heads = q_ref.shape[1]
    n_pages_step = len(k_refs)
    q = q_ref[0]
    lane = lax.broadcasted_iota(jnp.int32, q.shape, 1)
    first = lane < dh
    qm = jnp.concatenate([jnp.where(first, q, 0.0), jnp.where(first, 0.0, q)], axis=0).astype(BF16)
    s = jnp.concatenate([_dot_nt(qm, kr[0].astype(BF16)) for kr in k_refs], axis=1)
    head_mask = n_heads - 1
    own = ((lax.broadcasted_iota(jnp.int32, s.shape, 1) & head_mask)
           == (lax.broadcasted_iota(jnp.int32, s.shape, 0) & head_mask))
    s = jnp.where(own, s, NEG_INF)
    m_old = jnp.where(first_group, NEG_INF, m_ref[...])
    l_old = jnp.where(first_group, 0.0, l_ref[...])
    acc_old = jnp.where(first_group, 0.0, acc_ref[...])
    m_new = jnp.maximum(m_old, jnp.max(s, axis=-1, keepdims=True))
    alpha = jnp.exp2(m_old - m_new)
    pr = jnp.exp2(s - m_new)
    l_new = alpha * l_old + jnp.sum(pr, axis=-1, keepdims=True)
    pb = pr.astype(BF16)
    rows = k_refs[0].shape[1]
    pv = _dot(pb[:, :rows], v_refs[0][0].astype(BF16))
    for u in range(1, n_pages_step):
        pv += _dot(pb[:, u * rows:(u + 1) * rows], v_refs[u][0].astype(BF16))
    acc_new = alpha * acc_old + pv
    m_ref[...] = m_new
    l_ref[...] = l_new
    acc_ref[...] = acc_new

    @pl.when(last_group)
    def _():
        kn, vn = kn_ref[0], vn_ref[0]
        prod = q * kn
        sn = jnp.concatenate([jnp.sum(jnp.where(first, prod, 0.0), axis=-1, keepdims=True),
                              jnp.sum(jnp.where(first, 0.0, prod), axis=-1, keepdims=True)], axis=0)
        m_fin = jnp.maximum(m_new, sn)
        a2 = jnp.exp2(m_new - m_fin)
        pn = jnp.exp2(sn - m_fin)
        l_fin = a2 * l_new + pn
        acc_fin = a2 * acc_new + pn * jnp.concatenate([vn, vn], axis=0)
        o2 = acc_fin / l_fin
        o = o2[:n_heads] - lam * o2[n_heads:]
        o_ref[0] = _subln(o, sub_ref[...], lam_init).astype(BF16)


def _rotary_tables(start, length, dk, rows=None):
    angle = 1.0 / (ROPE_BASE ** jnp.linspace(0.0, 1.0, dk // 2, dtype=F32))
    angle = jnp.repeat(angle, 2)
    pos = start + jnp.arange(length, dtype=F32)
    th = pos[:, None] * angle[None, :]
    sign = jnp.where(jnp.arange(dk) % 2 == 0, -1.0, 1.0).astype(F32)
    cos_t = jnp.cos(th)
    sin_t = jnp.sin(th) * sign[None, :]
    if rows is not None:
        cos_t = jnp.broadcast_to(cos_t, (rows, cos_t.shape[1]))
        sin_t = jnp.broadcast_to(sin_t, (rows, sin_t.shape[1]))
    return cos_t, sin_t


def _rotary_half_tables(length, dk, block=128):
    angle = 1.0 / (ROPE_BASE ** jnp.linspace(0.0, 1.0, dk // 2, dtype=F32))
    n_hi = -(-length // block)
    th_hi = (jnp.arange(n_hi, dtype=F32) * block)[:, None] * angle[None, :]
    th_lo = jnp.arange(block, dtype=F32)[:, None] * angle[None, :]
    ch, sh = jnp.cos(th_hi)[:, None, :], jnp.sin(th_hi)[:, None, :]
    cl, sl = jnp.cos(th_lo)[None, :, :], jnp.sin(th_lo)[None, :, :]
    cos_t = (ch * cl - sh * sl).reshape(n_hi * block, dk // 2)[:length]
    sin_t = (sh * cl + ch * sl).reshape(n_hi * block, dk // 2)[:length]
    return cos_t, sin_t


def _split_cols_kernel(w_ref, o_ref):
    dk = w_ref.shape[1]
    half = dk // 2
    src = lax.broadcasted_iota(jnp.int32, (dk, dk), 0)
    dst = lax.broadcasted_iota(jnp.int32, (dk, dk), 1)
    perm = (src == jnp.where(dst < half, 2 * dst, 2 * (dst - half) + 1)).astype(BF16)
    o_ref[...] = _dot(w_ref[...], perm).astype(BF16)


def _split_even_odd_columns(w, n_cols, dk):
    rows = w.shape[0]
    return pl.pallas_call(
        _split_cols_kernel,
        grid=(n_cols // dk,),
        in_specs=[pl.BlockSpec((rows, dk), lambda c: (0, c))],
        out_specs=pl.BlockSpec((rows, dk), lambda c: (0, c)),
        out_shape=jax.ShapeDtypeStruct((rows, n_cols), BF16),
        compiler_params=_cparams("parallel"),
        name="split_cols",
    )(w)


def _merge_even_odd_rows(s, dk):
    lead = s.shape[:-2]
    return s.reshape(*lead, 2, dk // 2, s.shape[-1]).swapaxes(-3, -2).reshape(*lead, dk, s.shape[-1])


def _decay_tables(n_heads, c):
    lg = jnp.log1p(-jnp.exp2(-5.0 - jnp.arange(n_heads, dtype=F32)))
    idx = jnp.arange(c, dtype=F32)
    rel = idx[:, None] - idx[None, :]
    dec_in = jnp.where(rel[None] >= 0, jnp.exp(lg[:, None, None] * jnp.maximum(rel, 0.0)[None]), 0.0)
    dec_q = jnp.exp(lg[:, None] * (idx + 1.0)[None])[:, :, None]
    dec_k = jnp.exp(lg[:, None] * (c - 1.0 - idx)[None])[:, :, None]
    return dec_in, dec_q, dec_k


def kernel(x_prompt, x_sample, state_ret, cache_k, cache_v, page_table, p_prompt, p_sample, norm_mix, ret_w_in, ret_w_out, diff_w_in, diff_w_out, diff_lambda_q1, diff_lambda_k1, diff_lambda_q2, diff_lambda_k2, diff_subln, norm_ffn, w_up, w_down, ple_norm, w_ple_gate, w_ple_proj, final_norm):
    batch, seq, d = x_prompt.shape
    nb, dec_seq, _ = x_sample.shape
    assert dec_seq == 1
    depth = norm_mix.shape[0]
    mp, ms = batch * seq, nb * dec_seq
    yp = x_prompt.reshape(mp, d)
    ys = x_sample.reshape(ms, d)
    row = lambda a: a.reshape(1, -1)
    fn = row(final_norm)

    ret_p, ret_s, kp_rows, vp_rows, ks_rows, vs_rows = [], [], [], [], [], []
    for i in range(depth):
        g_mix = row(norm_mix[i])
        if i % N_MIXERS == 0:
            r = i // N_MIXERS
            n_heads = RET_HEADS
            dk = d // n_heads
            dv = 2 * dk
            w_in = ret_w_in[r].astype(BF16)
            w_out = ret_w_out[r].astype(BF16)
            k_scale = dk ** -0.5
            w_qk_split = _split_even_odd_columns(w_in, 2 * d, dk)
            cos_p, sin_p = _rotary_half_tables(seq, dk)
            proj_p = _ret_proj(yp, g_mix, w_qk_split, w_in, cos_p, sin_p, k_scale, dk, True)
            dec_in, dec_q, dec_k = _decay_tables(n_heads, RET_CHUNK)
            og_p, s_p = _ret_scan(proj_p, batch, seq, dec_in, dec_q, dec_k)
            yp = _matmul_res(og_p, w_out, yp)
            ret_p.append(_merge_even_odd_rows(s_p, dk).astype(x_prompt.dtype))
            cos_s, sin_s = _rotary_tables(PAST_LEN, 1, dk, rows=ms)
            proj_s = _ret_proj(ys, g_mix, w_in, w_in, cos_s, sin_s, k_scale, dk, False).astype(F32)
            gammas = tuple(math.exp(math.log1p(-2.0 ** (-5.0 - h))) for h in range(n_heads))
            q_col = proj_s[:, :d].reshape(ms, n_heads, 1, dk)
            k_col = proj_s[:, d:2 * d].reshape(ms, n_heads, 1, dk)
            v_row = proj_s[:, 2 * d:2 * d + n_heads * dv].reshape(ms, n_heads, 1, dv)
            g_row = proj_s[:, 2 * d + n_heads * dv:].reshape(ms, n_heads, 1, dv)
            og_s, s_s = _ret_step(q_col, k_col, v_row, g_row, state_ret[r].astype(F32), gammas)
            ys = _matmul_res(og_s.reshape(ms, n_heads * dv), w_out, ys)
            ret_s.append(s_s.astype(state_ret.dtype))
        else:
            di = i // N_MIXERS
            n_heads = DIFF_HEADS
            dv = d // n_heads
            dh = dv // 2
            lam_init = 0.8 - 0.6 * math.exp(-0.3 * i)
            w_in = diff_w_in[di].astype(BF16)
            w_out = diff_w_out[di].astype(BF16)
            lams = [row(a[di].astype(F32)) for a in (diff_lambda_q1, diff_lambda_k1, diff_lambda_q2, diff_lambda_k2)]
            sub = row(diff_subln[di].astype(F32))
            q_scale = dh ** -0.5 * math.log2(math.e)
            t = ATTN_BLOCK if seq % ATTN_BLOCK == 0 else seq
            q_p, kf_p, vf_p, kb_p, vt_p = _diff_proj(yp, g_mix, w_in, q_scale, t, t // 2)
            q_s, kf_s, vf_s = _diff_proj(ys, g_mix, w_in, q_scale, ms)
            n_phys, page_size = cache_k.shape[1], cache_k.shape[2]
            ck = cache_k[di].reshape(n_phys, page_size * n_heads, dv)
            cv = cache_v[di].reshape(n_phys, page_size * n_heads, dv)
            o_p, o_s = _diff_attn(q_p, kb_p, vt_p, batch, seq, t,
                                  q_s.astype(F32).reshape(ms, n_heads, dv), kf_s.reshape(ms, n_heads, dv),
                                  vf_s.reshape(ms, n_heads, dv), ck, cv, page_table, *lams, sub, lam_init)
            yp = _matmul_res(o_p, w_out, yp)
            kp_rows.append(kf_p.reshape(batch, seq, n_heads, dv))
            vp_rows.append(vf_p.reshape(batch, seq, n_heads, dv))
            ys = _matmul_res(o_s.reshape(ms, d), w_out, ys)
            ks_rows.append(kf_s.reshape(nb, dec_seq, n_heads, dv))
            vs_rows.append(vf_s.reshape(nb, dec_seq, n_heads, dv))
        g_ffn = row(norm_ffn[i])
        wu, wd = w_up[i].astype(BF16), w_down[i].astype(BF16)
        yp = _ffn(yp, g_ffn, wu, wd)
        ys = _ffn(ys, g_ffn, wu, wd)
        g_ple = row(ple_norm[i])
        wg, wp = w_ple_gate[i].astype(BF16), w_ple_proj[i].astype(BF16)
        last = i == depth - 1
        yp = _ple(yp, p_prompt.reshape(depth, mp, -1), i, g_ple, wg, wp, fn, last)
        ys = _ple(ys, p_sample.reshape(depth, ms, -1), i, g_ple, wg, wp, fn, last)

    y_prompt = yp.reshape(batch, seq, d)
    y_sample = ys.reshape(nb, dec_seq, d)
    return (y_prompt, y_sample, jnp.stack(ret_p), jnp.stack(ret_s),
            jnp.stack(kp_rows), jnp.stack(vp_rows), jnp.stack(ks_rows), jnp.stack(vs_rows))
```
